```python
import jax, jax.numpy as jnp
from jax import lax
import numpy as np

D_MODEL = 1024
BATCH = 16
SEQ = 2048
DEPTH = 4

NORM_EPS = 1e-6
ROPE_THETA = 10000.0
Q_BLOCK = 128

A_HEADS = 8
A_HEAD_DIM = 64
IDX_HEADS = 8
IDX_DIM = 64
DSA_TOPK_MAX = 256

B_HEADS = 8
B_HEAD_DIM = 64
MOBA_BLOCK = 256
MOBA_TOPK = 3
MOBA_Q_CHUNK = 16

EVEN_COLS = (A_HEADS * A_HEAD_DIM,
             A_HEAD_DIM,
             A_HEAD_DIM,
             IDX_HEADS * IDX_DIM,
             IDX_DIM,
             IDX_HEADS,
             B_HEADS * B_HEAD_DIM,
             B_HEADS * B_HEAD_DIM,
             B_HEADS * B_HEAD_DIM)
EVEN_IN_COLS = int(sum(EVEN_COLS))
EVEN_SPLITS = tuple(int(c) for c in np.cumsum(EVEN_COLS)[:-1])
EVEN_OUT_COLS = A_HEADS * A_HEAD_DIM + B_HEADS * B_HEAD_DIM

C_HEADS = 16
C_NOPE = 64
C_ROPE = 32
C_V = 64
C_Q_RANK = 768
C_KV_RANK = 256
C_IN_COLS = C_Q_RANK + C_KV_RANK + C_ROPE

PEER_HEADS = 8
PEER_NKEYS = 128
PEER_N_EXPERTS = PEER_NKEYS * PEER_NKEYS
PEER_QDIM = 256
PEER_TOPK_HALF = 16
PEER_TOPK = 16
PEER_CHUNK = 128

N_EVEN = (DEPTH + 1) // 2
N_ODD = DEPTH // 2

kernel_name = 'hybrid_dsa_moba_mla_peer'


def rms_norm(x, g):
    xf = x.astype(jnp.float32)
    y = xf * lax.rsqrt(jnp.mean(xf * xf, axis=-1, keepdims=True) + NORM_EPS)
    return (y * g.astype(jnp.float32)).astype(x.dtype)


def rope_tables(positions, dim):
    inv = 1.0 / (ROPE_THETA ** (jnp.arange(0, dim, 2, dtype=jnp.float32) / dim))
    ang = positions.astype(jnp.float32)[..., None] * inv
    return jnp.cos(ang)[:, :, None, :], jnp.sin(ang)[:, :, None, :]


def apply_rope(x, cos, sin):
    x1, x2 = jnp.split(x, 2, axis=-1)
    c = cos.astype(x.dtype)
    s = sin.astype(x.dtype)
    return jnp.concatenate([x1 * c - x2 * s, x1 * s + x2 * c], axis=-1)


def to_blocks(a, blk):
    b, s = a.shape[:2]
    return jnp.moveaxis(a.reshape((b, s // blk, blk) + a.shape[2:]), 1, 0)


def from_blocks(a):
    a = jnp.moveaxis(a, 0, 1)
    return a.reshape((a.shape[0], a.shape[1] * a.shape[2]) + a.shape[3:])


def dsa_attention(q, k, v, iq, ik, iw):
    b, s = q.shape[:2]
    n_sel = min(DSA_TOPK_MAX, s // 4)
    key_pos = jnp.arange(s)
    w_scale = (IDX_HEADS ** -0.5) * (IDX_DIM ** -0.5)
    a_scale = A_HEAD_DIM ** -0.5
    gather = jax.vmap(lambda t, i: t[i])

    def one_block(args):
        blk, qb, iqb, iwb = args
        q_pos = blk * Q_BLOCK + jnp.arange(Q_BLOCK)
        causal = key_pos[None, :] <= q_pos[:, None]
        logits = jnp.einsum('bqhd,bsd->bqhs', iqb, ik).astype(jnp.float32)
        score = jnp.einsum('bqhs,bqh->bqs', jax.nn.relu(logits), iwb.astype(jnp.float32)) * w_scale
        score = jnp.where(causal[None], score, -jnp.inf)
        _, sel = lax.top_k(score, n_sel)
        k_sel = gather(k, sel)
        v_sel = gather(v, sel)
        valid = sel <= q_pos[None, :, None]
        sc = jnp.einsum('bqhd,bqnd->bqhn', qb, k_sel).astype(jnp.float32) * a_scale
        sc = jnp.where(valid[:, :, None, :], sc, -jnp.inf)
        p = jax.nn.softmax(sc, axis=-1).astype(v.dtype)
        return jnp.einsum('bqhn,bqnd->bqhd', p, v_sel)

    nblk = s // Q_BLOCK
    out = lax.map(one_block, (jnp.arange(nblk), to_blocks(q, Q_BLOCK),
                              to_blocks(iq, Q_BLOCK), to_blocks(iw, Q_BLOCK)))
    return from_blocks(out)


def moba_attention(q, k, v):
    b, s, h, d = q.shape
    nb = -(-s // MOBA_BLOCK)
    pad = nb * MOBA_BLOCK - s

    def blocks(a):
        a = jnp.pad(a, ((0, 0), (0, pad), (0, 0), (0, 0)))
        return a.reshape(b, nb, MOBA_BLOCK, h, d).transpose(0, 3, 1, 2, 4)

    kb = blocks(k)
    vb = blocks(v)
    k_mean = jnp.mean(kb.astype(jnp.float32), axis=3).astype(k.dtype)
    n_top = min(MOBA_TOPK, nb - 1)
    scale = d ** -0.5
    blk_ids = jnp.arange(nb)
    key_in_blk = jnp.arange(MOBA_BLOCK)
    gather = jax.vmap(jax.vmap(lambda t, i: t[i]))

    def one_chunk(args):
        c, qc = args
        q_pos = c * MOBA_Q_CHUNK + jnp.arange(MOBA_Q_CHUNK)
        own = (c * MOBA_Q_CHUNK) // MOBA_BLOCK
        k_own = lax.dynamic_index_in_dim(kb, own, axis=2, keepdims=False)
        v_own = lax.dynamic_index_in_dim(vb, own, axis=2, keepdims=False)
        s_own = jnp.einsum('bqhd,bhkd->bhqk', qc, k_own).astype(jnp.float32) * scale
        s_own = jnp.where((own * MOBA_BLOCK + key_in_blk)[None, :] <= q_pos[:, None], s_own, -jnp.inf)
        if n_top == 0:
            p = jax.nn.softmax(s_own, axis=-1).astype(v.dtype)
            return jnp.einsum('bhqk,bhkd->bqhd', p, v_own)
        gate = jnp.einsum('bqhd,bhnd->bhqn', qc, k_mean).astype(jnp.float32)
        gate = jnp.where(blk_ids < own, gate, -jnp.inf)
        _, sel = lax.top_k(gate, n_top)
        k_sel = gather(kb, sel)
        v_sel = gather(vb, sel)
        s_past = jnp.einsum('bqhd,bhqnkd->bhqnk', qc, k_sel).astype(jnp.float32) * scale
        s_past = jnp.where((sel < own)[..., None], s_past, -jnp.inf)
        s_past = s_past.reshape(b, h, MOBA_Q_CHUNK, n_top * MOBA_BLOCK)
        p = jax.nn.softmax(jnp.concatenate([s_past, s_own], axis=-1), axis=-1).astype(v.dtype)
        p_past = p[..., :n_top * MOBA_BLOCK].reshape(b, h, MOBA_Q_CHUNK, n_top, MOBA_BLOCK)
        p_own = p[..., n_top * MOBA_BLOCK:]
        return (jnp.einsum('bhqnk,bhqnkd->bqhd', p_past, v_sel)
                + jnp.einsum('bhqk,bhkd->bqhd', p_own, v_own))

    nchunk = s // MOBA_Q_CHUNK
    out = lax.map(one_chunk, (jnp.arange(nchunk), to_blocks(q, MOBA_Q_CHUNK)))
    return from_blocks(out)


def even_mixer(h, cos64, sin64, w_in, w_out):
    b, s, _ = h.shape
    qa, ka, va, iq, ik, iw, qb, kb, vb = jnp.split(h @ w_in, EVEN_SPLITS, axis=-1)
    qa = apply_rope(qa.reshape(b, s, A_HEADS, A_HEAD_DIM), cos64, sin64)
    ka = apply_rope(ka[:, :, None, :], cos64, sin64)[:, :, 0]
    iq = apply_rope(iq.reshape(b, s, IDX_HEADS, IDX_DIM), cos64, sin64)
    ik = apply_rope(ik[:, :, None, :], cos64, sin64)[:, :, 0]
    out_a = dsa_attention(qa, ka, va, iq, ik, iw)
    qb = apply_rope(qb.reshape(b, s, B_HEADS, B_HEAD_DIM), cos64, sin64)
    kb = apply_rope(kb.reshape(b, s, B_HEADS, B_HEAD_DIM), cos64, sin64)
    vb = vb.reshape(b, s, B_HEADS, B_HEAD_DIM)
    out_b = moba_attention(qb, kb, vb)
    o = jnp.concatenate([out_a.reshape(b, s, A_HEADS * A_HEAD_DIM),
                         out_b.reshape(b, s, B_HEADS * B_HEAD_DIM)], axis=-1)
    return o @ w_out


def mla_attention(q_nope, q_rope, k_nope, k_rope, v):
    s = q_nope.shape[1]
    key_pos = jnp.arange(s)
    scale = (C_NOPE + C_ROPE) ** -0.5

    def one_block(args):
        blk, qn, qr = args
        q_pos = blk * Q_BLOCK + jnp.arange(Q_BLOCK)
        sc = (jnp.einsum('bqhd,bshd->bhqs', qn, k_nope)
              + jnp.einsum('bqhd,bsd->bhqs', qr, k_rope)).astype(jnp.float32) * scale
        sc = jnp.where(key_pos[None, :] <= q_pos[:, None], sc, -jnp.inf)
        p = jax.nn.softmax(sc, axis=-1).astype(v.dtype)
        return jnp.einsum('bhqs,bshd->bqhd', p, v)

    nblk = s // Q_BLOCK
    out = lax.map(one_block, (jnp.arange(nblk), to_blocks(q_nope, Q_BLOCK), to_blocks(q_rope, Q_BLOCK)))
    return from_blocks(out)


def mla_mixer(h, cos32, sin32, w_in, q_norm, kv_norm, w_uq, w_ukv, w_out):
    b, s, _ = h.shape
    cq, ckv, kr = jnp.split(h @ w_in, [C_Q_RANK, C_Q_RANK + C_KV_RANK], axis=-1)
    q = (rms_norm(cq, q_norm) @ w_uq).reshape(b, s, C_HEADS, C_NOPE + C_ROPE)
    q_nope, q_rope = jnp.split(q, [C_NOPE], axis=-1)
    q_rope = apply_rope(q_rope, cos32, sin32)
    kv = (rms_norm(ckv, kv_norm) @ w_ukv).reshape(b, s, C_HEADS, C_NOPE + C_V)
    k_nope, v = jnp.split(kv, [C_NOPE], axis=-1)
    k_rope = apply_rope(kr[:, :, None, :], cos32, sin32)[:, :, 0]
    o = mla_attention(q_nope, q_rope, k_nope, k_rope, v)
    return o.reshape(b, s, C_HEADS * C_V) @ w_out


def peer_ffn(h, w_q, sub_keys, u_tab, v_tab):
    b, s, d = h.shape
    tokens = h.reshape(b * s // PEER_CHUNK, PEER_CHUNK, d)

    def one_chunk(xc):
        t = xc.shape[0]
        q = (xc @ w_q).reshape(t, PEER_HEADS, 2, PEER_QDIM // 2)
        sc = jnp.einsum('thpc,hpnc->thpn', q, sub_keys).astype(jnp.float32)
        v_half, i_half = lax.top_k(sc, PEER_TOPK_HALF)
        cand = (v_half[:, :, 0, :, None] + v_half[:, :, 1, None, :]).reshape(t, PEER_HEADS, -1)
        cand_idx = (i_half[:, :, 0, :, None] * PEER_NKEYS + i_half[:, :, 1, None, :]).reshape(t, PEER_HEADS, -1)
        top_s, pos = lax.top_k(cand, PEER_TOPK)
        expert = jnp.take_along_axis(cand_idx, pos, axis=-1)
        g = jax.nn.softmax(top_s, axis=-1)
        u = u_tab[expert]
        act = jax.nn.gelu(jnp.einsum('thkd,td->thk', u, xc).astype(jnp.float32), approximate=False)
        w = (g * act).astype(xc.dtype)
        return jnp.einsum('thk,thkd->td', w, v_tab[expert])

    return lax.map(one_chunk, tokens).reshape(b, s, d)


def setup_inputs(seed: int = 0) -> dict:
    key = jax.random.key(seed)
    ks = jax.random.split(key, 20)
    f32 = jnp.float32

    def nrm(k, shape, scale):
        return jax.random.normal(k, shape, f32) * scale

    def gain(k, shape):
        return 1.0 + 0.02 * jax.random.normal(k, shape, f32)

    return {
        'x': jax.random.normal(ks[0], (BATCH, SEQ, D_MODEL), f32),
        'positions': jnp.broadcast_to(jnp.arange(SEQ, dtype=jnp.int32)[None, :], (BATCH, SEQ)),
        'attn_norm': gain(ks[1], (DEPTH, D_MODEL)),
        'ffn_norm': gain(ks[2], (DEPTH, D_MODEL)),
        'final_norm': gain(ks[3], (D_MODEL,)),
        'hyb_w_in': nrm(ks[4], (N_EVEN, D_MODEL, EVEN_IN_COLS), D_MODEL ** -0.5),
        'hyb_w_out': nrm(ks[5], (N_EVEN, EVEN_OUT_COLS, D_MODEL), EVEN_OUT_COLS ** -0.5),
        'mla_w_in': nrm(ks[6], (N_ODD, D_MODEL, C_IN_COLS), D_MODEL ** -0.5),
        'mla_q_norm': gain(ks[7], (N_ODD, C_Q_RANK)),
        'mla_kv_norm': gain(ks[8], (N_ODD, C_KV_RANK)),
        'mla_w_uq': nrm(ks[9], (N_ODD, C_Q_RANK, C_HEADS * (C_NOPE + C_ROPE)), C_Q_RANK ** -0.5),
        'mla_w_ukv': nrm(ks[10], (N_ODD, C_KV_RANK, C_HEADS * (C_NOPE + C_V)), C_KV_RANK ** -0.5),
        'mla_w_out': nrm(ks[11], (N_ODD, C_HEADS * C_V, D_MODEL), (C_HEADS * C_V) ** -0.5),
        'peer_w_q': nrm(ks[12], (DEPTH, D_MODEL, PEER_HEADS * PEER_QDIM), D_MODEL ** -0.5),
        'peer_sub_keys': nrm(ks[13], (DEPTH, PEER_HEADS, 2, PEER_NKEYS, PEER_QDIM // 2), (PEER_QDIM // 2) ** -0.5),
        'peer_u': nrm(ks[14], (DEPTH, PEER_N_EXPERTS, D_MODEL), D_MODEL ** -0.5),
        'peer_v': nrm(ks[15], (DEPTH, PEER_N_EXPERTS, D_MODEL), D_MODEL ** -0.5),
    }


def reference(x, positions, attn_norm, ffn_norm, final_norm, hyb_w_in, hyb_w_out,
              mla_w_in, mla_q_norm, mla_kv_norm, mla_w_uq, mla_w_ukv, mla_w_out,
              peer_w_q, peer_sub_keys, peer_u, peer_v):
    cos64, sin64 = rope_tables(positions, A_HEAD_DIM)
    cos32, sin32 = rope_tables(positions, C_ROPE)
    for i in range(DEPTH):
        h = rms_norm(x, attn_norm[i])
        j = i // 2
        if i % 2 == 0:
            x = x + even_mixer(h, cos64, sin64, hyb_w_in[j], hyb_w_out[j])
        else:
            x = x + mla_mixer(h, cos32, sin32, mla_w_in[j], mla_q_norm[j], mla_kv_norm[j],
                              mla_w_uq[j], mla_w_ukv[j], mla_w_out[j])
        h = rms_norm(x, ffn_norm[i])
        x = x + peer_ffn(h, peer_w_q[i], peer_sub_keys[i], peer_u[i], peer_v[i])
    return rms_norm(x, final_norm)
```

```python
import functools

import numpy as np
import jax
import jax.numpy as jnp
from jax import lax
from jax.experimental import pallas as pl
from jax.experimental.pallas import tpu as pltpu

NORM_EPS = 1e-6
ROPE_THETA = 10000.0

A_HEADS = 8
A_HEAD_DIM = 64
IDX_HEADS = 8
IDX_DIM = 64
DSA_TOPK_MAX = 256

B_HEADS = 8
B_HEAD_DIM = 64
MOBA_BLOCK = 256
MOBA_TOPK = 3

C_HEADS = 16
C_NOPE = 64
C_ROPE = 32
C_V = 64
C_Q_RANK = 768
C_KV_RANK = 256

PEER_HEADS = 8
PEER_NKEYS = 128
PEER_QDIM = 256
PEER_TOPK_HALF = 16
PEER_TOPK = 16

LANES = 128
VMEM_LIMIT = 56 * 1024 * 1024

F32 = jnp.float32
BF16 = jnp.bfloat16
I32 = jnp.int32
INT_MIN = -(2 ** 31)


def _params(*sem):
    return pltpu.CompilerParams(dimension_semantics=sem, vmem_limit_bytes=VMEM_LIMIT)


def _dot(a, b):
    return jnp.dot(a, b, preferred_element_type=F32)


def _dot_nt(a, b):
    return lax.dot_general(a, b, (((1,), (1,)), ((), ())), preferred_element_type=F32)


def _rope_table_kernel(pos_ref, inv_ref, sign_ref, cos_ref, sin_ref):
    ang = pos_ref[...].astype(F32) * inv_ref[...]
    cos_ref[...] = jnp.cos(ang)
    sin_ref[...] = jnp.sin(ang) * sign_ref[...]


def _rope_tables(pos, dim, tm=1024):
    n = pos.shape[0]
    half = dim // 2
    inv = 1.0 / (ROPE_THETA ** (jnp.arange(0, dim, 2, dtype=F32) / dim))
    reps = LANES // dim
    inv_l = jnp.tile(jnp.concatenate([inv, inv]), reps)[None, :]
    sign_l = jnp.tile(jnp.concatenate([-jnp.ones(half, F32), jnp.ones(half, F32)]), reps)[None, :]
    out = jax.ShapeDtypeStruct((n, LANES), F32)
    return pl.pallas_call(
        _rope_table_kernel,
        grid=(n // tm,),
        in_specs=[pl.BlockSpec((tm, 1), lambda i: (i, 0)),
                  pl.BlockSpec((1, LANES), lambda i: (0, 0)),
                  pl.BlockSpec((1, LANES), lambda i: (0, 0))],
        out_specs=[pl.BlockSpec((tm, LANES), lambda i: (i, 0))] * 2,
        out_shape=[out, out],
        compiler_params=_params("parallel"),
        name="rope_tables",
    )(pos, inv_l, sign_l)


def _rope_epilogue(acc, cos, sin, half):
    wd = acc.shape[1]
    reps = wd // LANES
    if reps > 1:
        cos = jnp.concatenate([cos] * reps, axis=1)
        sin = jnp.concatenate([sin] * reps, axis=1)
    lane = lax.broadcasted_iota(I32, acc.shape, 1)
    first = (lane % (2 * half)) < half
    partner = jnp.where(first, pltpu.roll(acc, wd - half, 1), pltpu.roll(acc, half, 1))
    return acc * cos + partner * sin


def _proj_kernel(*refs, groups, has_gain, emit_h, tab_names):
    it = iter(refs)
    x_ref = next(it)
    g_ref = next(it) if has_gain else None
    w_ref = next(it)
    tabs = {name: (next(it), next(it)) for name in tab_names}
    out_refs = [next(it) for _ in groups]
    h_ref = next(it) if emit_h else None

    x = x_ref[...]
    if has_gain:
        ms = jnp.mean(x * x, axis=-1, keepdims=True)
        h = x * lax.rsqrt(ms + NORM_EPS) * g_ref[...]
    else:
        h = x
    if emit_h:
        h_ref[...] = h
    hb = h.astype(BF16)
    for o_ref, (c0, wd, rope) in zip(out_refs, groups):
        acc = _dot(hb, w_ref[:, c0:c0 + wd])
        if rope is not None:
            cos_ref, sin_ref = tabs[rope]
            acc = _rope_epilogue(acc, cos_ref[...], sin_ref[...], 32 if rope == "r64" else 16)
        o_ref[...] = acc


def _fused_proj(x, gain, w, groups, tables, emit_h=False, tm=256):
    n, k = x.shape
    tab_names = sorted({g[2] for g in groups if g[2] is not None})
    args = [x]
    in_specs = [pl.BlockSpec((tm, k), lambda i: (i, 0))]
    if gain is not None:
        args.append(gain.reshape(1, k).astype(F32))
        in_specs.append(pl.BlockSpec((1, k), lambda i: (0, 0)))
    args.append(w)
    in_specs.append(pl.BlockSpec(w.shape, lambda i: (0, 0)))
    for name in tab_names:
        for t in tables[name]:
            args.append(t)
            in_specs.append(pl.BlockSpec((tm, LANES), lambda i: (i, 0)))
    out_shape = [jax.ShapeDtypeStruct((n, wd), F32) for (_, wd, _) in groups]
    out_specs = [pl.BlockSpec((tm, wd), lambda i: (i, 0)) for (_, wd, _) in groups]
    if emit_h:
        out_shape.append(jax.ShapeDtypeStruct((n, k), F32))
        out_specs.append(pl.BlockSpec((tm, k), lambda i: (i, 0)))
    kern = functools.partial(_proj_kernel, groups=tuple(groups), has_gain=gain is not None,
                             emit_h=emit_h, tab_names=tuple(tab_names))
    return pl.pallas_call(
        kern, grid=(n // tm,), in_specs=in_specs, out_specs=out_specs, out_shape=out_shape,
        compiler_params=_params("parallel"), name="fused_proj",
    )(*args)


def _outproj_kernel(*refs, n_in):
    res_ref = refs[0]
    a_refs = refs[1:1 + n_in]
    w_refs = refs[1 + n_in:1 + 2 * n_in]
    o_ref = refs[-1]
    acc = res_ref[...]
    for a_ref, w_ref in zip(a_refs, w_refs):
        acc = acc + _dot(a_ref[...].astype(BF16), w_ref[...])
    o_ref[...] = acc


def _outproj_residual(res, acts, ws, tm=256):
    n, d = res.shape
    in_specs = [pl.BlockSpec((tm, d), lambda i: (i, 0))]
    in_specs += [pl.BlockSpec((tm, a.shape[1]), lambda i: (i, 0)) for a in acts]
    in_specs += [pl.BlockSpec(w.shape, lambda i: (0, 0)) for w in ws]
    return pl.pallas_call(
        functools.partial(_outproj_kernel, n_in=len(acts)),
        grid=(n // tm,), in_specs=in_specs,
        out_specs=pl.BlockSpec((tm, d), lambda i: (i, 0)),
        out_shape=jax.ShapeDtypeStruct((n, d), F32),
        compiler_params=_params("parallel"), name="outproj_residual",
    )(res, *acts, *ws)


def _rmsnorm_kernel(x_ref, g_ref, o_ref):
    x = x_ref[...]
    ms = jnp.mean(x * x, axis=-1, keepdims=True)
    o_ref[...] = x * lax.rsqrt(ms + NORM_EPS) * g_ref[...]


def _rmsnorm(x, gain, tm=512):
    n, d = x.shape
    return pl.pallas_call(
        _rmsnorm_kernel, grid=(n // tm,),
        in_specs=[pl.BlockSpec((tm, d), lambda i: (i, 0)), pl.BlockSpec((1, d), lambda i: (0, 0))],
        out_specs=pl.BlockSpec((tm, d), lambda i: (i, 0)),
        out_shape=jax.ShapeDtypeStruct((n, d), F32),
        compiler_params=_params("parallel"), name="final_rmsnorm",
    )(x, gain.reshape(1, d))


def _softmax_pv(s, mask, v_bf):
    s = jnp.where(mask, s, -jnp.inf)
    m = jnp.max(s, axis=-1, keepdims=True)
    p = jnp.exp(s - m)
    l = jnp.sum(p, axis=-1, keepdims=True)
    return _dot(p.astype(BF16), v_bf) / l


def _dsa_kernel(iq_ref, qa_ref, kaik_ref, va_ref, iw_ref, o_ref, *, n_sel, tq, s_len):
    i = pl.program_id(1)
    kaik = kaik_ref[...]
    ka = kaik[:, 0:A_HEAD_DIM].astype(BF16)
    ik = kaik[:, A_HEAD_DIM:A_HEAD_DIM + IDX_DIM].astype(BF16)
    va = va_ref[:, 0:A_HEAD_DIM].astype(BF16)
    iw = iw_ref[:, A_HEAD_DIM:A_HEAD_DIM + IDX_HEADS]
    iq = iq_ref[...]

    score = jnp.zeros((tq, s_len), F32)
    for h in range(IDX_HEADS):
        lg = _dot_nt(iq[:, h * IDX_DIM:(h + 1) * IDX_DIM].astype(BF16), ik)
        score = score + jnp.maximum(lg, 0.0) * iw[:, h:h + 1]
    w_scale = (IDX_HEADS ** -0.5) * (IDX_DIM ** -0.5)
    score = score * w_scale + 0.0

    key_pos = lax.broadcasted_iota(I32, (tq, s_len), 1)
    q_pos = i * tq + lax.broadcasted_iota(I32, (tq, s_len), 0)
    causal = key_pos <= q_pos

    bits = pltpu.bitcast(score, I32)
    key = jnp.where(bits < 0, bits ^ 0x7FFFFFFF, bits)
    key = jnp.where(causal, key, INT_MIN)

    def count_ge(cand):
        return jnp.sum(jnp.where(key >= cand, 1.0, 0.0), axis=-1, keepdims=True)

    lo = jnp.where(count_ge(jnp.zeros((tq, 1), I32)) >= n_sel, 0, INT_MIN).astype(I32)

    def bisect(t, lo):
        cand = lo + jnp.left_shift(jnp.int32(1), 30 - t)
        return jnp.where(count_ge(cand) >= n_sel, cand, lo)

    thr = lax.fori_loop(0, 31, bisect, lo)

    gt = key > thr
    eq = key == thr
    need = n_sel - jnp.sum(jnp.where(gt, 1.0, 0.0), axis=-1, keepdims=True)
    r = lax.broadcasted_iota(I32, (LANES, LANES), 0)
    c = lax.broadcasted_iota(I32, (LANES, LANES), 1)
    tri = jnp.where(r < c, 1.0, 0.0).astype(BF16)
    eq_bf = jnp.where(eq, 1.0, 0.0).astype(BF16)
    carry = jnp.zeros((tq, 1), F32)
    pref = []
    for cb in range(s_len // LANES):
        blk = eq_bf[:, cb * LANES:(cb + 1) * LANES]
        pref.append(_dot(blk, tri) + carry)
        carry = carry + jnp.sum(blk.astype(F32), axis=-1, keepdims=True)
    prefix = jnp.concatenate(pref, axis=1)
    sel = (gt | (eq & (prefix < need))) & causal

    a_scale = A_HEAD_DIM ** -0.5
    qa = qa_ref[...]
    outs = []
    for h in range(A_HEADS):
        sc = _dot_nt(qa[:, h * A_HEAD_DIM:(h + 1) * A_HEAD_DIM].astype(BF16), ka) * a_scale
        outs.append(_softmax_pv(sc, sel, va))
    o_ref[...] = jnp.concatenate(outs, axis=1)


def _dsa_attention(iq, qa, kaik, vaiw, batch, s_len, tq=128):
    n = iq.shape[0]
    nq = s_len // tq
    n_sel = min(DSA_TOPK_MAX, s_len // 4)
    hd = A_HEADS * A_HEAD_DIM
    kern = functools.partial(_dsa_kernel, n_sel=n_sel, tq=tq, s_len=s_len)
    return pl.pallas_call(
        kern, grid=(batch, nq),
        in_specs=[pl.BlockSpec((tq, IDX_HEADS * IDX_DIM), lambda b, i: (b * nq + i, 0)),
                  pl.BlockSpec((tq, hd), lambda b, i: (b * nq + i, 0)),
                  pl.BlockSpec((s_len, LANES), lambda b, i: (b, 0)),
                  pl.BlockSpec((s_len, LANES), lambda b, i: (b, 0)),
                  pl.BlockSpec((tq, LANES), lambda b, i: (b * nq + i, 0))],
        out_specs=pl.BlockSpec((tq, hd), lambda b, i: (b * nq + i, 0)),
        out_shape=jax.ShapeDtypeStruct((n, hd), F32),
        compiler_params=_params("parallel", "arbitrary"), name="dsa_attention",
    )(iq, qa, kaik, vaiw, vaiw)


def _moba_kernel(q_ref, k_ref, v_ref, o_ref, *, n_top, s_len):
    i = pl.program_id(2)
    tq = MOBA_BLOCK
    nb = s_len // MOBA_BLOCK
    d = B_HEAD_DIM
    scale = d ** -0.5
    key_pos = lax.broadcasted_iota(I32, (tq, s_len), 1)
    q_pos = i * tq + lax.broadcasted_iota(I32, (tq, s_len), 0)
    own_mask = (key_pos >= i * MOBA_BLOCK) & (key_pos <= q_pos)
    blk_lane = lax.broadcasted_iota(I32, (tq, nb), 1)
    expand = jnp.where(lax.broadcasted_iota(I32, (nb, s_len), 1) // MOBA_BLOCK
                       == lax.broadcasted_iota(I32, (nb, s_len), 0), 1.0, 0.0).astype(BF16)
    outs = []
    for hh in range(LANES // d):
        q = q_ref[:, hh * d:(hh + 1) * d].astype(BF16)
        k = k_ref[:, hh * d:(hh + 1) * d]
        v = v_ref[:, hh * d:(hh + 1) * d].astype(BF16)
        k_mean = jnp.sum(k.reshape(nb, MOBA_BLOCK, d), axis=1) / MOBA_BLOCK
        gate = _dot_nt(q, k_mean.astype(BF16))
        gate = jnp.where(blk_lane < i, gate, -jnp.inf)
        rank = jnp.zeros((tq, nb), F32)
        for b2 in range(nb):
            gb = gate[:, b2:b2 + 1]
            beats = (gb > gate) | ((gb == gate) & (b2 < blk_lane))
            rank = rank + jnp.where(beats, 1.0, 0.0)
        sel = jnp.where((rank < n_top) & (blk_lane < i), 1.0, 0.0).astype(BF16)
        selk = _dot(sel, expand)
        mask = (selk > 0.5) | own_mask
        s = _dot_nt(q, k.astype(BF16)) * scale
        outs.append(_softmax_pv(s, mask, v))
    o_ref[...] = jnp.concatenate(outs, axis=1)


def _moba_attention(qb, kb, vb, batch, s_len):
    n = qb.shape[0]
    nb = s_len // MOBA_BLOCK
    n_top = min(MOBA_TOPK, nb - 1)
    hp = B_HEADS * B_HEAD_DIM // LANES
    kern = functools.partial(_moba_kernel, n_top=n_top, s_len=s_len)
    return pl.pallas_call(
        kern, grid=(batch, hp, nb),
        in_specs=[pl.BlockSpec((MOBA_BLOCK, LANES), lambda b, g, i: (b * nb + i, g)),
                  pl.BlockSpec((s_len, LANES), lambda b, g, i: (b, g)),
                  pl.BlockSpec((s_len, LANES), lambda b, g, i: (b, g))],
        out_specs=pl.BlockSpec((MOBA_BLOCK, LANES), lambda b, g, i: (b * nb + i, g)),
        out_shape=jax.ShapeDtypeStruct((n, B_HEADS * B_HEAD_DIM), F32),
        compiler_params=_params("parallel", "parallel", "arbitrary"), name="moba_attention",
    )(qb, kb, vb)


MLA_GROUP = 4


def _mla_kernel(qn_ref, qr_ref, kn_ref, kr_ref, v_ref, o_ref, *, tq, s_len):
    i = pl.program_id(2)
    scale = (C_NOPE + C_ROPE) ** -0.5
    key_pos = lax.broadcasted_iota(I32, (tq, s_len), 1)
    q_pos = i * tq + lax.broadcasted_iota(I32, (tq, s_len), 0)
    causal = key_pos <= q_pos
    kr = kr_ref[:, 0:C_ROPE].astype(BF16)
    outs = []
    for hh in range(MLA_GROUP):
        qn = qn_ref[:, hh * C_NOPE:(hh + 1) * C_NOPE].astype(BF16)
        qr = qr_ref[:, hh * C_ROPE:(hh + 1) * C_ROPE].astype(BF16)
        kn = kn_ref[:, hh * C_NOPE:(hh + 1) * C_NOPE].astype(BF16)
        v = v_ref[:, hh * C_V:(hh + 1) * C_V].astype(BF16)
        s = (_dot_nt(qn, kn) + _dot_nt(qr, kr)) * scale
        outs.append(_softmax_pv(s, causal, v))
    o_ref[...] = jnp.concatenate(outs, axis=1)


def _mla_attention(qn, qr, kn, kr, v, batch, s_len, tq=256):
    n = qn.shape[0]
    nq = s_len // tq
    ng = C_HEADS // MLA_GROUP
    kern = functools.partial(_mla_kernel, tq=tq, s_len=s_len)
    return pl.pallas_call(
        kern, grid=(batch, ng, nq),
        in_specs=[pl.BlockSpec((tq, MLA_GROUP * C_NOPE), lambda b, g, i: (b * nq + i, g)),
                  pl.BlockSpec((tq, MLA_GROUP * C_ROPE), lambda b, g, i: (b * nq + i, g)),
                  pl.BlockSpec((s_len, MLA_GROUP * C_NOPE), lambda b, g, i: (b, g)),
                  pl.BlockSpec((s_len, LANES), lambda b, g, i: (b, 0)),
                  pl.BlockSpec((s_len, MLA_GROUP * C_V), lambda b, g, i: (b, g))],
        out_specs=pl.BlockSpec((tq, MLA_GROUP * C_V), lambda b, g, i: (b * nq + i, g)),
        out_shape=jax.ShapeDtypeStruct((n, C_HEADS * C_V), F32),
        compiler_params=_params("parallel", "parallel", "arbitrary"), name="mla_attention",
    )(qn, qr, kn, kr, v)


def _topk_rows(sc, k):
    r = sc.shape[0]
    riota = lax.broadcasted_iota(I32, sc.shape, 0)
    vals, rows = [], []
    for _ in range(k):
        m = jnp.max(sc, axis=0, keepdims=True)
        pos = jnp.min(jnp.where(sc == m, riota, r), axis=0, keepdims=True)
        sc = jnp.where(riota == pos, -jnp.inf, sc)
        vals.append(m)
        rows.append(pos)
    return jnp.concatenate(vals, axis=0), jnp.concatenate(rows, axis=0)


def _peer_topk_kernel(q_ref, keys_ref, e_ref, g_ref):
    t = q_ref.shape[0]
    half_v, half_i = [], []
    for p in range(2):
        qh = q_ref[:, p * PEER_NKEYS:(p + 1) * PEER_NKEYS].astype(BF16)
        sc = _dot_nt(keys_ref[p], qh)
        v, ix = _topk_rows(sc, PEER_TOPK_HALF)
        half_v.append(v)
        half_i.append(ix)
    kh = PEER_TOPK_HALF
    cand = (half_v[0][:, None, :] + half_v[1][None, :, :]).reshape(kh * kh, t)
    cand_idx = (half_i[0][:, None, :] * PEER_NKEYS + half_i[1][None, :, :]).reshape(kh * kh, t)
    top_s, pos = _topk_rows(cand, PEER_TOPK)
    piota = lax.broadcasted_iota(I32, (kh * kh, t), 0)
    experts = [jnp.max(jnp.where(piota == pos[j:j + 1, :], cand_idx, -1), axis=0, keepdims=True)
               for j in range(PEER_TOPK)]
    e_ref[...] = jnp.concatenate(experts, axis=0)
    ex = jnp.exp(top_s - top_s[0:1, :])
    g_ref[...] = ex / jnp.sum(ex, axis=0, keepdims=True)


def _peer_topk(q, keys_bf, t=128):
    n = q.shape[0]
    nt = n // t
    rows = PEER_HEADS * PEER_TOPK
    return pl.pallas_call(
        _peer_topk_kernel, grid=(nt, PEER_HEADS),
        in_specs=[pl.BlockSpec((t, PEER_QDIM), lambda i, h: (i, h)),
                  pl.BlockSpec((None, 2, PEER_NKEYS, PEER_QDIM // 2), lambda i, h: (h, 0, 0, 0))],
        out_specs=[pl.BlockSpec((None, PEER_TOPK, t), lambda i, h: (i, h, 0))] * 2,
        out_shape=[jax.ShapeDtypeStruct((nt, rows, t), I32), jax.ShapeDtypeStruct((nt, rows, t), F32)],
        compiler_params=_params("parallel", "arbitrary"), name="peer_topk",
    )(q, keys_bf)


PEER_TB = 8
PEER_SEL = PEER_HEADS * PEER_TOPK


def _gather_copy(tab_hbm, rows, sem, e, t, j):
    return pltpu.make_async_copy(tab_hbm.at[pl.ds(e, 1), :], rows.at[t, pl.ds(j, 1), :], sem)


def _gather_rows(idx_ref, tab_hbm, rows, sem):
    for t in range(PEER_TB):
        def issue(j, carry, t=t):
            _gather_copy(tab_hbm, rows, sem, idx_ref[t, j], t, j).start()
            return carry
        lax.fori_loop(0, PEER_SEL, issue, 0, unroll=8)
    for t in range(PEER_TB):
        def drain(j, carry, t=t):
            _gather_copy(tab_hbm, rows, sem, 0, t, j).wait()
            return carry
        lax.fori_loop(0, PEER_SEL, drain, 0, unroll=8)


def _peer_u_kernel(idx_ref, h_ref, g_ref, u_hbm, w_ref, rows, sem):
    _gather_rows(idx_ref, u_hbm, rows, sem)
    h = h_ref[...].astype(BF16)
    acts = [_dot_nt(h[t:t + 1, :], rows[t].astype(BF16)) for t in range(PEER_TB)]
    act = jnp.concatenate(acts, axis=0)
    w_ref[...] = g_ref[...] * (0.5 * act * (1.0 + lax.erf(act * (2.0 ** -0.5))))


def _peer_v_kernel(idx_ref, w_ref, x_ref, v_hbm, o_ref, rows, sem):
    _gather_rows(idx_ref, v_hbm, rows, sem)
    w = w_ref[...].astype(BF16)
    outs = [_dot(w[t:t + 1, :], rows[t].astype(BF16)) for t in range(PEER_TB)]
    o_ref[...] = x_ref[...] + jnp.concatenate(outs, axis=0)


def _peer_gather_u(idx, h, g, u_tab):
    n, d = h.shape
    return pl.pallas_call(
        _peer_u_kernel, grid=(n // PEER_TB,),
        in_specs=[pl.BlockSpec((PEER_TB, PEER_SEL), lambda i: (i, 0), memory_space=pltpu.SMEM),
                  pl.BlockSpec((PEER_TB, d), lambda i: (i, 0)),
                  pl.BlockSpec((PEER_TB, PEER_SEL), lambda i: (i, 0)),
                  pl.BlockSpec(memory_space=pl.ANY)],
        out_specs=pl.BlockSpec((PEER_TB, PEER_SEL), lambda i: (i, 0)),
        out_shape=jax.ShapeDtypeStruct((n, PEER_SEL), F32),
        scratch_shapes=[pltpu.VMEM((PEER_TB, PEER_SEL, d), F32), pltpu.SemaphoreType.DMA],
        compiler_params=_params("arbitrary"), name="peer_gather_u",
    )(idx, h, g, u_tab)


def _peer_gather_v(idx, w, x, v_tab):
    n, d = x.shape
    return pl.pallas_call(
        _peer_v_kernel, grid=(n // PEER_TB,),
        in_specs=[pl.BlockSpec((PEER_TB, PEER_SEL), lambda i: (i, 0), memory_space=pltpu.SMEM),
                  pl.BlockSpec((PEER_TB, PEER_SEL), lambda i: (i, 0)),
                  pl.BlockSpec((PEER_TB, d), lambda i: (i, 0)),
                  pl.BlockSpec(memory_space=pl.ANY)],
        out_specs=pl.BlockSpec((PEER_TB, d), lambda i: (i, 0)),
        out_shape=jax.ShapeDtypeStruct((n, d), F32),
        scratch_shapes=[pltpu.VMEM((PEER_TB, PEER_SEL, d), F32), pltpu.SemaphoreType.DMA],
        compiler_params=_params("arbitrary"), name="peer_gather_v",
    )(idx, w, x, v_tab)


def _pad_cols(w, width):
    return jnp.pad(w, ((0, 0), (0, width - w.shape[1])))


def _even_weights(w_in):
    hd = A_HEADS * A_HEAD_DIM
    cols = np.cumsum([0, hd, A_HEAD_DIM, A_HEAD_DIM, IDX_HEADS * IDX_DIM, IDX_DIM, IDX_HEADS,
                      B_HEADS * B_HEAD_DIM, B_HEADS * B_HEAD_DIM, B_HEADS * B_HEAD_DIM])
    qa, ka, va, iq, ik, iw, qb, kb, vb = [w_in[:, cols[j]:cols[j + 1]] for j in range(9)]
    vaiw = _pad_cols(jnp.concatenate([va, iw], axis=1), LANES)
    w = jnp.concatenate([qa, iq, qb, kb, vb, ka, ik, vaiw], axis=1).astype(BF16)
    groups = [(0, 512, "r64"), (512, 512, "r64"), (1024, 512, "r64"), (1536, 512, "r64"),
              (2048, 512, None), (2560, LANES, "r64"), (2560 + LANES, LANES, None)]
    return w, groups


def _even_mixer(x, gain, w_in, w_out, tables, batch, s_len):
    w, groups = _even_weights(w_in)
    qa, iq, qb, kb, vb, kaik, vaiw = _fused_proj(x, gain, w, groups, tables)
    out_a = _dsa_attention(iq, qa, kaik, vaiw, batch, s_len)
    out_b = _moba_attention(qb, kb, vb, batch, s_len)
    hd = A_HEADS * A_HEAD_DIM
    w_out = w_out.astype(BF16)
    return _outproj_residual(x, [out_a, out_b], [w_out[:hd], w_out[hd:]])


def _mla_mixer(x, gain, w_in, q_norm, kv_norm, w_uq, w_ukv, w_out, tables, batch, s_len):
    w1 = _pad_cols(w_in, C_Q_RANK + C_KV_RANK + LANES).astype(BF16)
    groups1 = [(0, C_Q_RANK, None), (C_Q_RANK, C_KV_RANK, None), (C_Q_RANK + C_KV_RANK, LANES, "r32")]
    cq, ckv, kr = _fused_proj(x, gain, w1, groups1, tables)
    qd = C_NOPE + C_ROPE
    nope_cols = np.concatenate([np.arange(h * qd, h * qd + C_NOPE) for h in range(C_HEADS)])
    rope_cols = np.concatenate([np.arange(h * qd + C_NOPE, (h + 1) * qd) for h in range(C_HEADS)])
    w2 = jnp.concatenate([w_uq[:, nope_cols], w_uq[:, rope_cols]], axis=1).astype(BF16)
    groups2 = [(0, C_HEADS * C_NOPE, None), (C_HEADS * C_NOPE, C_HEADS * C_ROPE, "r32")]
    qn, qr = _fused_proj(cq, q_norm, w2, groups2, tables)
    kd = C_NOPE + C_V
    kn_cols = np.concatenate([np.arange(h * kd, h * kd + C_NOPE) for h in range(C_HEADS)])
    v_cols = np.concatenate([np.arange(h * kd + C_NOPE, (h + 1) * kd) for h in range(C_HEADS)])
    w3 = jnp.concatenate([w_ukv[:, kn_cols], w_ukv[:, v_cols]], axis=1).astype(BF16)
    groups3 = [(0, C_HEADS * C_NOPE, None), (C_HEADS * C_NOPE, C_HEADS * C_V, None)]
    kn, v = _fused_proj(ckv, kv_norm, w3, groups3, tables)
    o = _mla_attention(qn, qr, kn, kr, v, batch, s_len)
    return _outproj_residual(x, [o], [w_out.astype(BF16)])


def _peer_ffn(x, gain, w_q, sub_keys, u_tab, v_tab):
    n, d = x.shape
    q, h = _fused_proj(x, gain, w_q.astype(BF16), [(0, PEER_HEADS * PEER_QDIM, None)], {}, emit_h=True)
    experts, gates = _peer_topk(q, sub_keys.astype(BF16))
    idx = jnp.swapaxes(experts, 1, 2).reshape(n, PEER_SEL)
    g = jnp.swapaxes(gates, 1, 2).reshape(n, PEER_SEL)
    w = _peer_gather_u(idx, h, g, u_tab)
    return _peer_gather_v(idx, w, x, v_tab)


def kernel(x, positions, attn_norm, ffn_norm, final_norm, hyb_w_in, hyb_w_out, mla_w_in, mla_q_norm,
           mla_kv_norm, mla_w_uq, mla_w_ukv, mla_w_out, peer_w_q, peer_sub_keys, peer_u, peer_v):
    batch, s_len, d = x.shape
    n = batch * s_len
    depth = attn_norm.shape[0]
    pos = positions.reshape(n, 1).astype(I32)
    tables = {"r64": _rope_tables(pos, A_HEAD_DIM), "r32": _rope_tables(pos, C_ROPE)}
    xf = x.reshape(n, d)
    for i in range(depth):
        j = i // 2
        if i % 2 == 0:
            xf = _even_mixer(xf, attn_norm[i], hyb_w_in[j], hyb_w_out[j], tables, batch, s_len)
        else:
            xf = _mla_mixer(xf, attn_norm[i], mla_w_in[j], mla_q_norm[j], mla_kv_norm[j], mla_w_uq[j],
                            mla_w_ukv[j], mla_w_out[j], tables, batch, s_len)
        xf = _peer_ffn(xf, ffn_norm[i], peer_w_q[i], peer_sub_keys[i], peer_u[i], peer_v[i])
    return _rmsnorm(xf, final_norm).reshape(batch, s_len, d)
```

```python
import functools

import numpy as np
import jax
import jax.numpy as jnp
from jax import lax
from jax.experimental import pallas as pl
from jax.experimental.pallas import tpu as pltpu
from jax.experimental.pallas import tpu_sc as plsc

NORM_EPS = 1e-6
ROPE_THETA = 10000.0

A_HEADS = 8
A_HEAD_DIM = 64
IDX_HEADS = 8
IDX_DIM = 64
DSA_TOPK_MAX = 256

B_HEADS = 8
B_HEAD_DIM = 64
MOBA_BLOCK = 256
MOBA_TOPK = 3

C_HEADS = 16
C_NOPE = 64
C_ROPE = 32
C_V = 64
C_Q_RANK = 768
C_KV_RANK = 256

PEER_HEADS = 8
PEER_NKEYS = 128
PEER_QDIM = 256
PEER_TOPK_HALF = 16
PEER_TOPK = 16

LANES = 128
VMEM_LIMIT = 56 * 1024 * 1024

F32 = jnp.float32
BF16 = jnp.bfloat16
I32 = jnp.int32
INT_MIN = -(2 ** 31)


def _params(*sem):
    return pltpu.CompilerParams(dimension_semantics=sem, vmem_limit_bytes=VMEM_LIMIT)


def _dot(a, b):
    return jnp.dot(a, b, preferred_element_type=F32)


def _dot_nt(a, b):
    return lax.dot_general(a, b, (((1,), (1,)), ((), ())), preferred_element_type=F32)


def _rope_table_kernel(pos_ref, inv_ref, sign_ref, cos_ref, sin_ref):
    ang = pos_ref[...].astype(F32) * inv_ref[...]
    cos_ref[...] = jnp.cos(ang)
    sin_ref[...] = jnp.sin(ang) * sign_ref[...]


def _rope_tables(pos, dim, tm=1024):
    n = pos.shape[0]
    half = dim // 2
    inv = 1.0 / (ROPE_THETA ** (jnp.arange(0, dim, 2, dtype=F32) / dim))
    reps = LANES // dim
    inv_l = jnp.tile(jnp.concatenate([inv, inv]), reps)[None, :]
    sign_l = jnp.tile(jnp.concatenate([-jnp.ones(half, F32), jnp.ones(half, F32)]), reps)[None, :]
    out = jax.ShapeDtypeStruct((n, LANES), F32)
    return pl.pallas_call(
        _rope_table_kernel,
        grid=(n // tm,),
        in_specs=[pl.BlockSpec((tm, 1), lambda i: (i, 0)),
                  pl.BlockSpec((1, LANES), lambda i: (0, 0)),
                  pl.BlockSpec((1, LANES), lambda i: (0, 0))],
        out_specs=[pl.BlockSpec((tm, LANES), lambda i: (i, 0))] * 2,
        out_shape=[out, out],
        compiler_params=_params("parallel"),
        name="rope_tables",
    )(pos, inv_l, sign_l)


def _rope_epilogue(acc, cos, sin, half):
    wd = acc.shape[1]
    reps = wd // LANES
    if reps > 1:
        cos = jnp.concatenate([cos] * reps, axis=1)
        sin = jnp.concatenate([sin] * reps, axis=1)
    lane = lax.broadcasted_iota(I32, acc.shape, 1)
    first = (lane % (2 * half)) < half
    partner = jnp.where(first, pltpu.roll(acc, wd - half, 1), pltpu.roll(acc, half, 1))
    return acc * cos + partner * sin


def _proj_kernel(*refs, groups, has_gain, emit_h, tab_names):
    it = iter(refs)
    x_ref = next(it)
    g_ref = next(it) if has_gain else None
    w_ref = next(it)
    tabs = {name: (next(it), next(it)) for name in tab_names}
    out_refs = [next(it) for _ in groups]
    h_ref = next(it) if emit_h else None

    x = x_ref[...]
    if has_gain:
        ms = jnp.mean(x * x, axis=-1, keepdims=True)
        h = x * lax.rsqrt(ms + NORM_EPS) * g_ref[...]
    else:
        h = x
    if emit_h:
        h_ref[...] = h
    hb = h.astype(BF16)
    for o_ref, (c0, wd, rope) in zip(out_refs, groups):
        acc = _dot(hb, w_ref[:, c0:c0 + wd])
        if rope is not None:
            cos_ref, sin_ref = tabs[rope]
            acc = _rope_epilogue(acc, cos_ref[...], sin_ref[...], 32 if rope == "r64" else 16)
        o_ref[...] = acc


def _fused_proj(x, gain, w, groups, tables, emit_h=False, tm=256):
    n, k = x.shape
    tab_names = sorted({g[2] for g in groups if g[2] is not None})
    args = [x]
    in_specs = [pl.BlockSpec((tm, k), lambda i: (i, 0))]
    if gain is not None:
        args.append(gain.reshape(1, k).astype(F32))
        in_specs.append(pl.BlockSpec((1, k), lambda i: (0, 0)))
    args.append(w)
    in_specs.append(pl.BlockSpec(w.shape, lambda i: (0, 0)))
    for name in tab_names:
        for t in tables[name]:
            args.append(t)
            in_specs.append(pl.BlockSpec((tm, LANES), lambda i: (i, 0)))
    out_shape = [jax.ShapeDtypeStruct((n, wd), F32) for (_, wd, _) in groups]
    out_specs = [pl.BlockSpec((tm, wd), lambda i: (i, 0)) for (_, wd, _) in groups]
    if emit_h:
        out_shape.append(jax.ShapeDtypeStruct((n, k), F32))
        out_specs.append(pl.BlockSpec((tm, k), lambda i: (i, 0)))
    kern = functools.partial(_proj_kernel, groups=tuple(groups), has_gain=gain is not None,
                             emit_h=emit_h, tab_names=tuple(tab_names))
    return pl.pallas_call(
        kern, grid=(n // tm,), in_specs=in_specs, out_specs=out_specs, out_shape=out_shape,
        compiler_params=_params("parallel"), name="fused_proj",
    )(*args)


def _outproj_kernel(*refs, n_in):
    res_ref = refs[0]
    a_refs = refs[1:1 + n_in]
    w_refs = refs[1 + n_in:1 + 2 * n_in]
    o_ref = refs[-1]
    acc = res_ref[...]
    for a_ref, w_ref in zip(a_refs, w_refs):
        acc = acc + _dot(a_ref[...].astype(BF16), w_ref[...])
    o_ref[...] = acc


def _outproj_residual(res, acts, ws, tm=256):
    n, d = res.shape
    in_specs = [pl.BlockSpec((tm, d), lambda i: (i, 0))]
    in_specs += [pl.BlockSpec((tm, a.shape[1]), lambda i: (i, 0)) for a in acts]
    in_specs += [pl.BlockSpec(w.shape, lambda i: (0, 0)) for w in ws]
    return pl.pallas_call(
        functools.partial(_outproj_kernel, n_in=len(acts)),
        grid=(n // tm,), in_specs=in_specs,
        out_specs=pl.BlockSpec((tm, d), lambda i: (i, 0)),
        out_shape=jax.ShapeDtypeStruct((n, d), F32),
        compiler_params=_params("parallel"), name="outproj_residual",
    )(res, *acts, *ws)


def _rmsnorm_kernel(x_ref, g_ref, o_ref):
    x = x_ref[...]
    ms = jnp.mean(x * x, axis=-1, keepdims=True)
    o_ref[...] = x * lax.rsqrt(ms + NORM_EPS) * g_ref[...]


def _rmsnorm(x, gain, tm=512):
    n, d = x.shape
    return pl.pallas_call(
        _rmsnorm_kernel, grid=(n // tm,),
        in_specs=[pl.BlockSpec((tm, d), lambda i: (i, 0)), pl.BlockSpec((1, d), lambda i: (0, 0))],
        out_specs=pl.BlockSpec((tm, d), lambda i: (i, 0)),
        out_shape=jax.ShapeDtypeStruct((n, d), F32),
        compiler_params=_params("parallel"), name="final_rmsnorm",
    )(x, gain.reshape(1, d))


def _softmax_pv(s, mask, v_bf):
    s = jnp.where(mask, s, -jnp.inf)
    m = jnp.max(s, axis=-1, keepdims=True)
    p = jnp.exp(s - m)
    l = jnp.sum(p, axis=-1, keepdims=True)
    return _dot(p.astype(BF16), v_bf) / l


def _dsa_kernel(iq_ref, qa_ref, kaik_ref, va_ref, iw_ref, o_ref, *, n_sel, tq, s_len):
    i = pl.program_id(1)
    kaik = kaik_ref[...]
    ka = kaik[:, 0:A_HEAD_DIM].astype(BF16)
    ik = kaik[:, A_HEAD_DIM:A_HEAD_DIM + IDX_DIM].astype(BF16)
    va = va_ref[:, 0:A_HEAD_DIM].astype(BF16)
    iw = iw_ref[:, A_HEAD_DIM:A_HEAD_DIM + IDX_HEADS]
    iq = iq_ref[...]

    score = jnp.zeros((tq, s_len), F32)
    for h in range(IDX_HEADS):
        lg = _dot_nt(iq[:, h * IDX_DIM:(h + 1) * IDX_DIM].astype(BF16), ik)
        score = score + jnp.maximum(lg, 0.0) * iw[:, h:h + 1]
    w_scale = (IDX_HEADS ** -0.5) * (IDX_DIM ** -0.5)
    score = score * w_scale + 0.0

    key_pos = lax.broadcasted_iota(I32, (tq, s_len), 1)
    q_pos = i * tq + lax.broadcasted_iota(I32, (tq, s_len), 0)
    causal = key_pos <= q_pos

    bits = pltpu.bitcast(score, I32)
    key = jnp.where(bits < 0, bits ^ 0x7FFFFFFF, bits)
    key = jnp.where(causal, key, INT_MIN)

    def count_ge(cand):
        return jnp.sum(jnp.where(key >= cand, 1.0, 0.0), axis=-1, keepdims=True)

    lo = jnp.where(count_ge(jnp.zeros((tq, 1), I32)) >= n_sel, 0, INT_MIN).astype(I32)

    def bisect(t, lo):
        cand = lo + jnp.left_shift(jnp.int32(1), 30 - t)
        return jnp.where(count_ge(cand) >= n_sel, cand, lo)

    thr = lax.fori_loop(0, 31, bisect, lo)

    gt = key > thr
    eq = key == thr
    need = n_sel - jnp.sum(jnp.where(gt, 1.0, 0.0), axis=-1, keepdims=True)
    r = lax.broadcasted_iota(I32, (LANES, LANES), 0)
    c = lax.broadcasted_iota(I32, (LANES, LANES), 1)
    tri = jnp.where(r < c, 1.0, 0.0).astype(BF16)
    eq_bf = jnp.where(eq, 1.0, 0.0).astype(BF16)
    carry = jnp.zeros((tq, 1), F32)
    pref = []
    for cb in range(s_len // LANES):
        blk = eq_bf[:, cb * LANES:(cb + 1) * LANES]
        pref.append(_dot(blk, tri) + carry)
        carry = carry + jnp.sum(blk.astype(F32), axis=-1, keepdims=True)
    prefix = jnp.concatenate(pref, axis=1)
    sel = (gt | (eq & (prefix < need))) & causal

    a_scale = A_HEAD_DIM ** -0.5
    qa = qa_ref[...]
    outs = []
    for h in range(A_HEADS):
        sc = _dot_nt(qa[:, h * A_HEAD_DIM:(h + 1) * A_HEAD_DIM].astype(BF16), ka) * a_scale
        outs.append(_softmax_pv(sc, sel, va))
    o_ref[...] = jnp.concatenate(outs, axis=1)


def _dsa_attention(iq, qa, kaik, vaiw, batch, s_len, tq=128):
    n = iq.shape[0]
    nq = s_len // tq
    n_sel = min(DSA_TOPK_MAX, s_len // 4)
    hd = A_HEADS * A_HEAD_DIM
    kern = functools.partial(_dsa_kernel, n_sel=n_sel, tq=tq, s_len=s_len)
    return pl.pallas_call(
        kern, grid=(batch, nq),
        in_specs=[pl.BlockSpec((tq, IDX_HEADS * IDX_DIM), lambda b, i: (b * nq + i, 0)),
                  pl.BlockSpec((tq, hd), lambda b, i: (b * nq + i, 0)),
                  pl.BlockSpec((s_len, LANES), lambda b, i: (b, 0)),
                  pl.BlockSpec((s_len, LANES), lambda b, i: (b, 0)),
                  pl.BlockSpec((tq, LANES), lambda b, i: (b * nq + i, 0))],
        out_specs=pl.BlockSpec((tq, hd), lambda b, i: (b * nq + i, 0)),
        out_shape=jax.ShapeDtypeStruct((n, hd), F32),
        compiler_params=_params("parallel", "arbitrary"), name="dsa_attention",
    )(iq, qa, kaik, vaiw, vaiw)


def _moba_kernel(q_ref, k_ref, v_ref, o_ref, *, n_top, s_len):
    i = pl.program_id(2)
    tq = MOBA_BLOCK
    nb = s_len // MOBA_BLOCK
    d = B_HEAD_DIM
    scale = d ** -0.5
    key_pos = lax.broadcasted_iota(I32, (tq, s_len), 1)
    q_pos = i * tq + lax.broadcasted_iota(I32, (tq, s_len), 0)
    own_mask = (key_pos >= i * MOBA_BLOCK) & (key_pos <= q_pos)
    blk_lane = lax.broadcasted_iota(I32, (tq, nb), 1)
    expand = jnp.where(lax.broadcasted_iota(I32, (nb, s_len), 1) // MOBA_BLOCK
                       == lax.broadcasted_iota(I32, (nb, s_len), 0), 1.0, 0.0).astype(BF16)
    outs = []
    for hh in range(LANES // d):
        q = q_ref[:, hh * d:(hh + 1) * d].astype(BF16)
        k = k_ref[:, hh * d:(hh + 1) * d]
        v = v_ref[:, hh * d:(hh + 1) * d].astype(BF16)
        k_mean = jnp.sum(k.reshape(nb, MOBA_BLOCK, d), axis=1) / MOBA_BLOCK
        gate = _dot_nt(q, k_mean.astype(BF16))
        gate = jnp.where(blk_lane < i, gate, -jnp.inf)
        rank = jnp.zeros((tq, nb), F32)
        for b2 in range(nb):
            gb = gate[:, b2:b2 + 1]
            beats = (gb > gate) | ((gb == gate) & (b2 < blk_lane))
            rank = rank + jnp.where(beats, 1.0, 0.0)
        sel = jnp.where((rank < n_top) & (blk_lane < i), 1.0, 0.0).astype(BF16)
        selk = _dot(sel, expand)
        mask = (selk > 0.5) | own_mask
        s = _dot_nt(q, k.astype(BF16)) * scale
        outs.append(_softmax_pv(s, mask, v))
    o_ref[...] = jnp.concatenate(outs, axis=1)


def _moba_attention(qb, kb, vb, batch, s_len):
    n = qb.shape[0]
    nb = s_len // MOBA_BLOCK
    n_top = min(MOBA_TOPK, nb - 1)
    hp = B_HEADS * B_HEAD_DIM // LANES
    kern = functools.partial(_moba_kernel, n_top=n_top, s_len=s_len)
    return pl.pallas_call(
        kern, grid=(batch, hp, nb),
        in_specs=[pl.BlockSpec((MOBA_BLOCK, LANES), lambda b, g, i: (b * nb + i, g)),
                  pl.BlockSpec((s_len, LANES), lambda b, g, i: (b, g)),
                  pl.BlockSpec((s_len, LANES), lambda b, g, i: (b, g))],
        out_specs=pl.BlockSpec((MOBA_BLOCK, LANES), lambda b, g, i: (b * nb + i, g)),
        out_shape=jax.ShapeDtypeStruct((n, B_HEADS * B_HEAD_DIM), F32),
        compiler_params=_params("parallel", "parallel", "arbitrary"), name="moba_attention",
    )(qb, kb, vb)


MLA_GROUP = 4


def _mla_kernel(qn_ref, qr_ref, kn_ref, kr_ref, v_ref, o_ref, *, tq, s_len):
    i = pl.program_id(2)
    scale = (C_NOPE + C_ROPE) ** -0.5
    key_pos = lax.broadcasted_iota(I32, (tq, s_len), 1)
    q_pos = i * tq + lax.broadcasted_iota(I32, (tq, s_len), 0)
    causal = key_pos <= q_pos
    kr = kr_ref[:, 0:C_ROPE].astype(BF16)
    outs = []
    for hh in range(MLA_GROUP):
        qn = qn_ref[:, hh * C_NOPE:(hh + 1) * C_NOPE].astype(BF16)
        qr = qr_ref[:, hh * C_ROPE:(hh + 1) * C_ROPE].astype(BF16)
        kn = kn_ref[:, hh * C_NOPE:(hh + 1) * C_NOPE].astype(BF16)
        v = v_ref[:, hh * C_V:(hh + 1) * C_V].astype(BF16)
        s = (_dot_nt(qn, kn) + _dot_nt(qr, kr)) * scale
        outs.append(_softmax_pv(s, causal, v))
    o_ref[...] = jnp.concatenate(outs, axis=1)


def _mla_attention(qn, qr, kn, kr, v, batch, s_len, tq=256):
    n = qn.shape[0]
    nq = s_len // tq
    ng = C_HEADS // MLA_GROUP
    kern = functools.partial(_mla_kernel, tq=tq, s_len=s_len)
    return pl.pallas_call(
        kern, grid=(batch, ng, nq),
        in_specs=[pl.BlockSpec((tq, MLA_GROUP * C_NOPE), lambda b, g, i: (b * nq + i, g)),
                  pl.BlockSpec((tq, MLA_GROUP * C_ROPE), lambda b, g, i: (b * nq + i, g)),
                  pl.BlockSpec((s_len, MLA_GROUP * C_NOPE), lambda b, g, i: (b, g)),
                  pl.BlockSpec((s_len, LANES), lambda b, g, i: (b, 0)),
                  pl.BlockSpec((s_len, MLA_GROUP * C_V), lambda b, g, i: (b, g))],
        out_specs=pl.BlockSpec((tq, MLA_GROUP * C_V), lambda b, g, i: (b * nq + i, g)),
        out_shape=jax.ShapeDtypeStruct((n, C_HEADS * C_V), F32),
        compiler_params=_params("parallel", "parallel", "arbitrary"), name="mla_attention",
    )(qn, qr, kn, kr, v)


def _topk_rows(sc, k):
    r = sc.shape[0]
    riota = lax.broadcasted_iota(I32, sc.shape, 0)
    vals, rows = [], []
    for _ in range(k):
        m = jnp.max(sc, axis=0, keepdims=True)
        pos = jnp.min(jnp.where(sc == m, riota, r), axis=0, keepdims=True)
        sc = jnp.where(riota == pos, -jnp.inf, sc)
        vals.append(m)
        rows.append(pos)
    return jnp.concatenate(vals, axis=0), jnp.concatenate(rows, axis=0)


def _peer_topk_kernel(q_ref, keys_ref, e_ref, g_ref):
    t = q_ref.shape[0]
    half_v, half_i = [], []
    for p in range(2):
        qh = q_ref[:, p * PEER_NKEYS:(p + 1) * PEER_NKEYS].astype(BF16)
        sc = _dot_nt(keys_ref[p], qh)
        v, ix = _topk_rows(sc, PEER_TOPK_HALF)
        half_v.append(v)
        half_i.append(ix)
    kh = PEER_TOPK_HALF
    cand = (half_v[0][:, None, :] + half_v[1][None, :, :]).reshape(kh * kh, t)
    cand_idx = (half_i[0][:, None, :] * PEER_NKEYS + half_i[1][None, :, :]).reshape(kh * kh, t)
    top_s, pos = _topk_rows(cand, PEER_TOPK)
    piota = lax.broadcasted_iota(I32, (kh * kh, t), 0)
    experts = [jnp.max(jnp.where(piota == pos[j:j + 1, :], cand_idx, -1), axis=0, keepdims=True)
               for j in range(PEER_TOPK)]
    e_ref[...] = jnp.concatenate(experts, axis=0)
    ex = jnp.exp(top_s - top_s[0:1, :])
    g_ref[...] = ex / jnp.sum(ex, axis=0, keepdims=True)


def _peer_topk(q, keys_bf, t=128):
    n = q.shape[0]
    nt = n // t
    rows = PEER_HEADS * PEER_TOPK
    return pl.pallas_call(
        _peer_topk_kernel, grid=(nt, PEER_HEADS),
        in_specs=[pl.BlockSpec((t, PEER_QDIM), lambda i, h: (i, h)),
                  pl.BlockSpec((None, 2, PEER_NKEYS, PEER_QDIM // 2), lambda i, h: (h, 0, 0, 0))],
        out_specs=[pl.BlockSpec((None, PEER_TOPK, t), lambda i, h: (i, h, 0))] * 2,
        out_shape=[jax.ShapeDtypeStruct((nt, rows, t), I32), jax.ShapeDtypeStruct((nt, rows, t), F32)],
        compiler_params=_params("parallel", "arbitrary"), name="peer_topk",
    )(q, keys_bf)


PEER_SEL = PEER_HEADS * PEER_TOPK

SC_CORES = 2
SC_SUBCORES = 16
SC_LANES = 16
SC_WORKERS = SC_CORES * SC_SUBCORES
SC_ROWS = 32
SC_TOK = 8
SC_CHUNKS = SC_TOK * PEER_SEL // SC_ROWS
SC_VCOLS = 256
SC_UNROLL = 4


def _sc_mesh():
    return plsc.VectorSubcoreMesh(core_axis_name="c", subcore_axis_name="s",
                                  num_cores=SC_CORES, num_subcores=SC_SUBCORES)


def _sc_chunk_pipeline(gather, compute):
    gather(0, 0).start()

    @pl.loop(0, SC_CHUNKS, step=2)
    def _(q):
        gather(q + 1, 1).start()
        gather(q, 0).wait()
        compute(q, 0)

        @pl.when(q + 2 < SC_CHUNKS)
        def _():
            gather(q + 2, 0).start()

        gather(q + 1, 1).wait()
        compute(q + 1, 1)


def _sc_u_body(idx_hbm, h_hbm, u_hbm, act_hbm, idx_v, h_v, rows_v, act_v, sem0, sem1, *, tpw, d):
    tok_base = (lax.axis_index("s") * SC_CORES + lax.axis_index("c")) * tpw
    sems = (sem0, sem1)

    def gather(q, b):
        return pltpu.make_async_copy(u_hbm.at[idx_v.at[pl.ds(q * SC_ROWS, SC_ROWS)]], rows_v.at[b], sems[b])

    def compute(q, b):
        hoff = (q // (PEER_SEL // SC_ROWS)) * d
        lane = lax.iota(I32, SC_LANES)
        for g in range(SC_ROWS // SC_LANES):
            def body(c, accs, g=g):
                xv = h_v[pl.ds(hoff + c * SC_LANES, SC_LANES)]
                return tuple(a + rows_v[b, g * SC_LANES + r, pl.ds(c * SC_LANES, SC_LANES)] * xv
                             for r, a in enumerate(accs))
            accs = lax.fori_loop(0, d // SC_LANES, body,
                                 tuple(jnp.zeros((SC_LANES,), F32) for _ in range(SC_LANES)),
                                 unroll=SC_UNROLL)
            out = jnp.zeros((SC_LANES,), F32)
            for r in range(SC_LANES):
                out = jnp.where(lane == r, jnp.sum(accs[r]), out)
            act_v[pl.ds(q * SC_ROWS + g * SC_LANES, SC_LANES)] = out

    @pl.loop(0, tpw // SC_TOK)
    def _(blk):
        tok0 = tok_base + blk * SC_TOK
        pltpu.sync_copy(idx_hbm.at[pl.ds(tok0 * PEER_SEL, SC_TOK * PEER_SEL)], idx_v)
        pltpu.sync_copy(h_hbm.at[pl.ds(tok0 * d, SC_TOK * d)], h_v)
        _sc_chunk_pipeline(gather, compute)
        pltpu.sync_copy(act_v, act_hbm.at[pl.ds(tok0 * PEER_SEL, SC_TOK * PEER_SEL)])


def _sc_peer_u(idx_flat, h_flat, u_tab):
    d = u_tab.shape[1]
    n = h_flat.shape[0] // d
    assert n % (SC_WORKERS * SC_TOK) == 0 and d % SC_VCOLS == 0
    body = functools.partial(_sc_u_body, tpw=n // SC_WORKERS, d=d)
    return pl.kernel(
        body,
        out_type=jax.ShapeDtypeStruct((n * PEER_SEL,), F32),
        mesh=_sc_mesh(),
        scratch_types=[pltpu.VMEM((SC_TOK * PEER_SEL,), I32),
                       pltpu.VMEM((SC_TOK * d,), F32),
                       pltpu.VMEM((2, SC_ROWS, d), F32),
                       pltpu.VMEM((SC_TOK * PEER_SEL,), F32),
                       pltpu.SemaphoreType.DMA, pltpu.SemaphoreType.DMA],
        compiler_params=pltpu.CompilerParams(needs_layout_passes=False),
        name="sc_peer_u",
    )(idx_flat, h_flat, u_tab)


def _sc_v_body(idx_hbm, w_hbm, x_hbm, v_hbm, o_hbm, idx_v, w_v, out_v, rows_v, sem0, sem1, *, tpw, d):
    tok_base = (lax.axis_index("s") * SC_CORES + lax.axis_index("c")) * tpw
    sems = (sem0, sem1)
    nacc = SC_VCOLS // SC_LANES

    def gather(q, b):
        return pltpu.make_async_copy(v_hbm.at[idx_v.at[pl.ds(q * SC_ROWS, SC_ROWS)]], rows_v.at[b], sems[b])

    def compute(q, b):
        ooff = (q // (PEER_SEL // SC_ROWS)) * d
        for cp in range(d // SC_VCOLS):
            col0 = cp * SC_VCOLS

            def body(r, accs, col0=col0):
                wv = plsc.load_gather(w_v, [jnp.full((SC_LANES,), q * SC_ROWS + r, I32)])
                return tuple(acc + wv * rows_v[b, r, pl.ds(col0 + a * SC_LANES, SC_LANES)]
                             for a, acc in enumerate(accs))

            accs = lax.fori_loop(
                0, SC_ROWS, body,
                tuple(out_v[pl.ds(ooff + col0 + a * SC_LANES, SC_LANES)] for a in range(nacc)))
            for a in range(nacc):
                out_v[pl.ds(ooff + col0 + a * SC_LANES, SC_LANES)] = accs[a]

    @pl.loop(0, tpw // SC_TOK)
    def _(blk):
        tok0 = tok_base + blk * SC_TOK
        pltpu.sync_copy(idx_hbm.at[pl.ds(tok0 * PEER_SEL, SC_TOK * PEER_SEL)], idx_v)
        pltpu.sync_copy(w_hbm.at[pl.ds(tok0 * PEER_SEL, SC_TOK * PEER_SEL)], w_v)
        pltpu.sync_copy(x_hbm.at[pl.ds(tok0 * d, SC_TOK * d)], out_v)
        _sc_chunk_pipeline(gather, compute)
        pltpu.sync_copy(out_v, o_hbm.at[pl.ds(tok0 * d, SC_TOK * d)])


def _sc_peer_v(idx_flat, w_flat, x_flat, v_tab):
    d = v_tab.shape[1]
    n = x_flat.shape[0] // d
    assert n % (SC_WORKERS * SC_TOK) == 0 and d % SC_VCOLS == 0
    body = functools.partial(_sc_v_body, tpw=n // SC_WORKERS, d=d)
    return pl.kernel(
        body,
        out_type=jax.ShapeDtypeStruct((n * d,), F32),
        mesh=_sc_mesh(),
        scratch_types=[pltpu.VMEM((SC_TOK * PEER_SEL,), I32),
                       pltpu.VMEM((SC_TOK * PEER_SEL,), F32),
                       pltpu.VMEM((SC_TOK * d,), F32),
                       pltpu.VMEM((2, SC_ROWS, d), F32),
                       pltpu.SemaphoreType.DMA, pltpu.SemaphoreType.DMA],
        compiler_params=pltpu.CompilerParams(needs_layout_passes=False),
        name="sc_peer_v",
    )(idx_flat, w_flat, x_flat, v_tab)


def _peer_gate_kernel(g_ref, act_ref, w_ref):
    act = act_ref[...]
    w_ref[...] = g_ref[...] * (0.5 * act * (1.0 + lax.erf(act * (2.0 ** -0.5))))


def _peer_gate(g, act, tm=2048):
    n = g.shape[0]
    spec = pl.BlockSpec((tm, PEER_SEL), lambda i: (i, 0))
    return pl.pallas_call(
        _peer_gate_kernel, grid=(n // tm,), in_specs=[spec, spec], out_specs=spec,
        out_shape=jax.ShapeDtypeStruct((n, PEER_SEL), F32),
        compiler_params=_params("parallel"), name="peer_gate",
    )(g, act)


def _pad_cols(w, width):
    return jnp.pad(w, ((0, 0), (0, width - w.shape[1])))


def _even_weights(w_in):
    hd = A_HEADS * A_HEAD_DIM
    cols = np.cumsum([0, hd, A_HEAD_DIM, A_HEAD_DIM, IDX_HEADS * IDX_DIM, IDX_DIM, IDX_HEADS,
                      B_HEADS * B_HEAD_DIM, B_HEADS * B_HEAD_DIM, B_HEADS * B_HEAD_DIM])
    qa, ka, va, iq, ik, iw, qb, kb, vb = [w_in[:, cols[j]:cols[j + 1]] for j in range(9)]
    vaiw = _pad_cols(jnp.concatenate([va, iw], axis=1), LANES)
    w = jnp.concatenate([qa, iq, qb, kb, vb, ka, ik, vaiw], axis=1).astype(BF16)
    groups = [(0, 512, "r64"), (512, 512, "r64"), (1024, 512, "r64"), (1536, 512, "r64"),
              (2048, 512, None), (2560, LANES, "r64"), (2560 + LANES, LANES, None)]
    return w, groups


def _even_mixer(x, gain, w_in, w_out, tables, batch, s_len):
    w, groups = _even_weights(w_in)
    qa, iq, qb, kb, vb, kaik, vaiw = _fused_proj(x, gain, w, groups, tables)
    out_a = _dsa_attention(iq, qa, kaik, vaiw, batch, s_len)
    out_b = _moba_attention(qb, kb, vb, batch, s_len)
    hd = A_HEADS * A_HEAD_DIM
    w_out = w_out.astype(BF16)
    return _outproj_residual(x, [out_a, out_b], [w_out[:hd], w_out[hd:]])


def _mla_mixer(x, gain, w_in, q_norm, kv_norm, w_uq, w_ukv, w_out, tables, batch, s_len):
    w1 = _pad_cols(w_in, C_Q_RANK + C_KV_RANK + LANES).astype(BF16)
    groups1 = [(0, C_Q_RANK, None), (C_Q_RANK, C_KV_RANK, None), (C_Q_RANK + C_KV_RANK, LANES, "r32")]
    cq, ckv, kr = _fused_proj(x, gain, w1, groups1, tables)
    qd = C_NOPE + C_ROPE
    nope_cols = np.concatenate([np.arange(h * qd, h * qd + C_NOPE) for h in range(C_HEADS)])
    rope_cols = np.concatenate([np.arange(h * qd + C_NOPE, (h + 1) * qd) for h in range(C_HEADS)])
    w2 = jnp.concatenate([w_uq[:, nope_cols], w_uq[:, rope_cols]], axis=1).astype(BF16)
    groups2 = [(0, C_HEADS * C_NOPE, None), (C_HEADS * C_NOPE, C_HEADS * C_ROPE, "r32")]
    qn, qr = _fused_proj(cq, q_norm, w2, groups2, tables)
    kd = C_NOPE + C_V
    kn_cols = np.concatenate([np.arange(h * kd, h * kd + C_NOPE) for h in range(C_HEADS)])
    v_cols = np.concatenate([np.arange(h * kd + C_NOPE, (h + 1) * kd) for h in range(C_HEADS)])
    w3 = jnp.concatenate([w_ukv[:, kn_cols], w_ukv[:, v_cols]], axis=1).astype(BF16)
    groups3 = [(0, C_HEADS * C_NOPE, None), (C_HEADS * C_NOPE, C_HEADS * C_V, None)]
    kn, v = _fused_proj(ckv, kv_norm, w3, groups3, tables)
    o = _mla_attention(qn, qr, kn, kr, v, batch, s_len)
    return _outproj_residual(x, [o], [w_out.astype(BF16)])


def _peer_ffn(x, gain, w_q, sub_keys, u_tab, v_tab):
    n, d = x.shape
    q, h = _fused_proj(x, gain, w_q.astype(BF16), [(0, PEER_HEADS * PEER_QDIM, None)], {}, emit_h=True)
    experts, gates = _peer_topk(q, sub_keys.astype(BF16))
    idx = jnp.swapaxes(experts, 1, 2).reshape(n, PEER_SEL)
    g = jnp.swapaxes(gates, 1, 2).reshape(n, PEER_SEL)
    idx_flat = idx.reshape(n * PEER_SEL)
    act = _sc_peer_u(idx_flat, h.reshape(n * d), u_tab)
    w = _peer_gate(g, act.reshape(n, PEER_SEL))
    return _sc_peer_v(idx_flat, w.reshape(n * PEER_SEL), x.reshape(n * d), v_tab).reshape(n, d)


def kernel(x, positions, attn_norm, ffn_norm, final_norm, hyb_w_in, hyb_w_out, mla_w_in, mla_q_norm,
           mla_kv_norm, mla_w_uq, mla_w_ukv, mla_w_out, peer_w_q, peer_sub_keys, peer_u, peer_v):
    batch, s_len, d = x.shape
    n = batch * s_len
    depth = attn_norm.shape[0]
    pos = positions.reshape(n, 1).astype(I32)
    tables = {"r64": _rope_tables(pos, A_HEAD_DIM), "r32": _rope_tables(pos, C_ROPE)}
    xf = x.reshape(n, d)
    for i in range(depth):
        j = i // 2
        if i % 2 == 0:
            xf = _even_mixer(xf, attn_norm[i], hyb_w_in[j], hyb_w_out[j], tables, batch, s_len)
        else:
            xf = _mla_mixer(xf, attn_norm[i], mla_w_in[j], mla_q_norm[j], mla_kv_norm[j], mla_w_uq[j],
                            mla_w_ukv[j], mla_w_out[j], tables, batch, s_len)
        xf = _peer_ffn(xf, ffn_norm[i], peer_w_q[i], peer_sub_keys[i], peer_u[i], peer_v[i])
    return _rmsnorm(xf, final_norm).reshape(batch, s_len, d)
```

```python
import functools

import numpy as np
import jax
import jax.numpy as jnp
from jax import lax
from jax.experimental import pallas as pl
from jax.experimental.pallas import tpu as pltpu
from jax.experimental.pallas import tpu_sc as plsc

NORM_EPS = 1e-6
ROPE_THETA = 10000.0

A_HEADS = 8
A_HEAD_DIM = 64
IDX_HEADS = 8
IDX_DIM = 64
DSA_TOPK_MAX = 256

B_HEADS = 8
B_HEAD_DIM = 64
MOBA_BLOCK = 256
MOBA_TOPK = 3

C_HEADS = 16
C_NOPE = 64
C_ROPE = 32
C_V = 64
C_Q_RANK = 768
C_KV_RANK = 256

PEER_HEADS = 8
PEER_NKEYS = 128
PEER_QDIM = 256
PEER_TOPK_HALF = 16
PEER_TOPK = 16

BATCH_GROUPS = 2
LANES = 128
VMEM_LIMIT = 56 * 1024 * 1024

F32 = jnp.float32
BF16 = jnp.bfloat16
I32 = jnp.int32
INT_MIN = -(2 ** 31)


def _params(*sem):
    return pltpu.CompilerParams(dimension_semantics=sem, vmem_limit_bytes=VMEM_LIMIT)


def _dot(a, b):
    return jnp.dot(a, b, preferred_element_type=F32)


def _dot_nt(a, b):
    return lax.dot_general(a, b, (((1,), (1,)), ((), ())), preferred_element_type=F32)


def _rope_table_kernel(pos_ref, inv_ref, sign_ref, cos_ref, sin_ref):
    ang = pos_ref[...].astype(F32) * inv_ref[...]
    cos_ref[...] = jnp.cos(ang)
    sin_ref[...] = jnp.sin(ang) * sign_ref[...]


def _rope_tables(pos, dim, tm=1024):
    n = pos.shape[0]
    half = dim // 2
    inv = 1.0 / (ROPE_THETA ** (jnp.arange(0, dim, 2, dtype=F32) / dim))
    reps = LANES // dim
    inv_l = jnp.tile(jnp.concatenate([inv, inv]), reps)[None, :]
    sign_l = jnp.tile(jnp.concatenate([-jnp.ones(half, F32), jnp.ones(half, F32)]), reps)[None, :]
    out = jax.ShapeDtypeStruct((n, LANES), F32)
    return pl.pallas_call(
        _rope_table_kernel,
        grid=(n // tm,),
        in_specs=[pl.BlockSpec((tm, 1), lambda i: (i, 0)),
                  pl.BlockSpec((1, LANES), lambda i: (0, 0)),
                  pl.BlockSpec((1, LANES), lambda i: (0, 0))],
        out_specs=[pl.BlockSpec((tm, LANES), lambda i: (i, 0))] * 2,
        out_shape=[out, out],
        compiler_params=_params("parallel"),
        name="rope_tables",
    )(pos, inv_l, sign_l)


def _rope_epilogue(acc, cos, sin, half):
    wd = acc.shape[1]
    reps = wd // LANES
    if reps > 1:
        cos = jnp.concatenate([cos] * reps, axis=1)
        sin = jnp.concatenate([sin] * reps, axis=1)
    lane = lax.broadcasted_iota(I32, acc.shape, 1)
    first = (lane % (2 * half)) < half
    partner = jnp.where(first, pltpu.roll(acc, wd - half, 1), pltpu.roll(acc, half, 1))
    return acc * cos + partner * sin


def _proj_kernel(*refs, groups, has_gain, emit_h, tab_names):
    it = iter(refs)
    x_ref = next(it)
    g_ref = next(it) if has_gain else None
    w_ref = next(it)
    tabs = {name: (next(it), next(it)) for name in tab_names}
    out_refs = [next(it) for _ in groups]
    h_ref = next(it) if emit_h else None

    x = x_ref[...]
    if has_gain:
        ms = jnp.mean(x * x, axis=-1, keepdims=True)
        h = x * lax.rsqrt(ms + NORM_EPS) * g_ref[...]
    else:
        h = x
    if emit_h:
        h_ref[...] = h
    hb = h.astype(BF16)
    for o_ref, (c0, wd, rope) in zip(out_refs, groups):
        acc = _dot(hb, w_ref[:, c0:c0 + wd])
        if rope is not None:
            cos_ref, sin_ref = tabs[rope]
            acc = _rope_epilogue(acc, cos_ref[...], sin_ref[...], 32 if rope == "r64" else 16)
        o_ref[...] = acc


def _fused_proj(x, gain, w, groups, tables, emit_h=False, tm=256):
    n, k = x.shape
    tab_names = sorted({g[2] for g in groups if g[2] is not None})
    args = [x]
    in_specs = [pl.BlockSpec((tm, k), lambda i: (i, 0))]
    if gain is not None:
        args.append(gain.reshape(1, k).astype(F32))
        in_specs.append(pl.BlockSpec((1, k), lambda i: (0, 0)))
    args.append(w)
    in_specs.append(pl.BlockSpec(w.shape, lambda i: (0, 0)))
    for name in tab_names:
        for t in tables[name]:
            args.append(t)
            in_specs.append(pl.BlockSpec((tm, LANES), lambda i: (i, 0)))
    out_shape = [jax.ShapeDtypeStruct((n, wd), F32) for (_, wd, _) in groups]
    out_specs = [pl.BlockSpec((tm, wd), lambda i: (i, 0)) for (_, wd, _) in groups]
    if emit_h:
        out_shape.append(jax.ShapeDtypeStruct((n, k), F32))
        out_specs.append(pl.BlockSpec((tm, k), lambda i: (i, 0)))
    kern = functools.partial(_proj_kernel, groups=tuple(groups), has_gain=gain is not None,
                             emit_h=emit_h, tab_names=tuple(tab_names))
    return pl.pallas_call(
        kern, grid=(n // tm,), in_specs=in_specs, out_specs=out_specs, out_shape=out_shape,
        compiler_params=_params("parallel"), name="fused_proj",
    )(*args)


def _outproj_kernel(*refs, n_in):
    res_ref = refs[0]
    a_refs = refs[1:1 + n_in]
    w_refs = refs[1 + n_in:1 + 2 * n_in]
    o_ref = refs[-1]
    acc = res_ref[...]
    for a_ref, w_ref in zip(a_refs, w_refs):
        acc = acc + _dot(a_ref[...].astype(BF16), w_ref[...])
    o_ref[...] = acc


def _outproj_residual(res, acts, ws, tm=256):
    n, d = res.shape
    in_specs = [pl.BlockSpec((tm, d), lambda i: (i, 0))]
    in_specs += [pl.BlockSpec((tm, a.shape[1]), lambda i: (i, 0)) for a in acts]
    in_specs += [pl.BlockSpec(w.shape, lambda i: (0, 0)) for w in ws]
    return pl.pallas_call(
        functools.partial(_outproj_kernel, n_in=len(acts)),
        grid=(n // tm,), in_specs=in_specs,
        out_specs=pl.BlockSpec((tm, d), lambda i: (i, 0)),
        out_shape=jax.ShapeDtypeStruct((n, d), F32),
        compiler_params=_params("parallel"), name="outproj_residual",
    )(res, *acts, *ws)


def _rmsnorm_kernel(x_ref, g_ref, o_ref):
    x = x_ref[...]
    ms = jnp.mean(x * x, axis=-1, keepdims=True)
    o_ref[...] = x * lax.rsqrt(ms + NORM_EPS) * g_ref[...]


def _rmsnorm(x, gain, tm=512):
    n, d = x.shape
    return pl.pallas_call(
        _rmsnorm_kernel, grid=(n // tm,),
        in_specs=[pl.BlockSpec((tm, d), lambda i: (i, 0)), pl.BlockSpec((1, d), lambda i: (0, 0))],
        out_specs=pl.BlockSpec((tm, d), lambda i: (i, 0)),
        out_shape=jax.ShapeDtypeStruct((n, d), F32),
        compiler_params=_params("parallel"), name="final_rmsnorm",
    )(x, gain.reshape(1, d))


def _softmax_pv(s, mask, v_bf):
    s = jnp.where(mask, s, -jnp.inf)
    m = jnp.max(s, axis=-1, keepdims=True)
    p = jnp.exp(s - m)
    l = jnp.sum(p, axis=-1, keepdims=True)
    return _dot(p.astype(BF16), v_bf) / l


def _dsa_kernel(iq_ref, qa_ref, kaik_ref, va_ref, iw_ref, o_ref, *, n_sel, tq, s_len):
    i = pl.program_id(1)
    kaik = kaik_ref[...]
    ka = kaik[:, 0:A_HEAD_DIM].astype(BF16)
    ik = kaik[:, A_HEAD_DIM:A_HEAD_DIM + IDX_DIM].astype(BF16)
    va = va_ref[:, 0:A_HEAD_DIM].astype(BF16)
    iw = iw_ref[:, A_HEAD_DIM:A_HEAD_DIM + IDX_HEADS]
    iq = iq_ref[...]

    score = jnp.zeros((tq, s_len), F32)
    for h in range(IDX_HEADS):
        lg = _dot_nt(iq[:, h * IDX_DIM:(h + 1) * IDX_DIM].astype(BF16), ik)
        score = score + jnp.maximum(lg, 0.0) * iw[:, h:h + 1]
    w_scale = (IDX_HEADS ** -0.5) * (IDX_DIM ** -0.5)
    score = score * w_scale + 0.0

    key_pos = lax.broadcasted_iota(I32, (tq, s_len), 1)
    q_pos = i * tq + lax.broadcasted_iota(I32, (tq, s_len), 0)
    causal = key_pos <= q_pos

    bits = pltpu.bitcast(score, I32)
    key = jnp.where(bits < 0, bits ^ 0x7FFFFFFF, bits)
    key = jnp.where(causal, key, INT_MIN)

    def count_ge(cand):
        return jnp.sum(jnp.where(key >= cand, 1.0, 0.0), axis=-1, keepdims=True)

    lo = jnp.where(count_ge(jnp.zeros((tq, 1), I32)) >= n_sel, 0, INT_MIN).astype(I32)

    def bisect(t, lo):
        cand = lo + jnp.left_shift(jnp.int32(1), 30 - t)
        return jnp.where(count_ge(cand) >= n_sel, cand, lo)

    thr = lax.fori_loop(0, 31, bisect, lo)

    gt = key > thr
    eq = key == thr
    need = n_sel - jnp.sum(jnp.where(gt, 1.0, 0.0), axis=-1, keepdims=True)
    r = lax.broadcasted_iota(I32, (LANES, LANES), 0)
    c = lax.broadcasted_iota(I32, (LANES, LANES), 1)
    tri = jnp.where(r < c, 1.0, 0.0).astype(BF16)
    eq_bf = jnp.where(eq, 1.0, 0.0).astype(BF16)
    carry = jnp.zeros((tq, 1), F32)
    pref = []
    for cb in range(s_len // LANES):
        blk = eq_bf[:, cb * LANES:(cb + 1) * LANES]
        pref.append(_dot(blk, tri) + carry)
        carry = carry + jnp.sum(blk.astype(F32), axis=-1, keepdims=True)
    prefix = jnp.concatenate(pref, axis=1)
    sel = (gt | (eq & (prefix < need))) & causal

    a_scale = A_HEAD_DIM ** -0.5
    qa = qa_ref[...]
    outs = []
    for h in range(A_HEADS):
        sc = _dot_nt(qa[:, h * A_HEAD_DIM:(h + 1) * A_HEAD_DIM].astype(BF16), ka) * a_scale
        outs.append(_softmax_pv(sc, sel, va))
    o_ref[...] = jnp.concatenate(outs, axis=1)


def _dsa_attention(iq, qa, kaik, vaiw, batch, s_len, tq=128):
    n = iq.shape[0]
    nq = s_len // tq
    n_sel = min(DSA_TOPK_MAX, s_len // 4)
    hd = A_HEADS * A_HEAD_DIM
    kern = functools.partial(_dsa_kernel, n_sel=n_sel, tq=tq, s_len=s_len)
    return pl.pallas_call(
        kern, grid=(batch, nq),
        in_specs=[pl.BlockSpec((tq, IDX_HEADS * IDX_DIM), lambda b, i: (b * nq + i, 0)),
                  pl.BlockSpec((tq, hd), lambda b, i: (b * nq + i, 0)),
                  pl.BlockSpec((s_len, LANES), lambda b, i: (b, 0)),
                  pl.BlockSpec((s_len, LANES), lambda b, i: (b, 0)),
                  pl.BlockSpec((tq, LANES), lambda b, i: (b * nq + i, 0))],
        out_specs=pl.BlockSpec((tq, hd), lambda b, i: (b * nq + i, 0)),
        out_shape=jax.ShapeDtypeStruct((n, hd), F32),
        compiler_params=_params("parallel", "arbitrary"), name="dsa_attention",
    )(iq, qa, kaik, vaiw, vaiw)


def _moba_kernel(q_ref, k_ref, v_ref, o_ref, *, n_top, s_len):
    i = pl.program_id(2)
    tq = MOBA_BLOCK
    nb = s_len // MOBA_BLOCK
    d = B_HEAD_DIM
    scale = d ** -0.5
    key_pos = lax.broadcasted_iota(I32, (tq, s_len), 1)
    q_pos = i * tq + lax.broadcasted_iota(I32, (tq, s_len), 0)
    own_mask = (key_pos >= i * MOBA_BLOCK) & (key_pos <= q_pos)
    blk_lane = lax.broadcasted_iota(I32, (tq, nb), 1)
    expand = jnp.where(lax.broadcasted_iota(I32, (nb, s_len), 1) // MOBA_BLOCK
                       == lax.broadcasted_iota(I32, (nb, s_len), 0), 1.0, 0.0).astype(BF16)
    outs = []
    for hh in range(LANES // d):
        q = q_ref[:, hh * d:(hh + 1) * d].astype(BF16)
        k = k_ref[:, hh * d:(hh + 1) * d]
        v = v_ref[:, hh * d:(hh + 1) * d].astype(BF16)
        k_mean = jnp.sum(k.reshape(nb, MOBA_BLOCK, d), axis=1) / MOBA_BLOCK
        gate = _dot_nt(q, k_mean.astype(BF16))
        gate = jnp.where(blk_lane < i, gate, -jnp.inf)
        rank = jnp.zeros((tq, nb), F32)
        for b2 in range(nb):
            gb = gate[:, b2:b2 + 1]
            beats = (gb > gate) | ((gb == gate) & (b2 < blk_lane))
            rank = rank + jnp.where(beats, 1.0, 0.0)
        sel = jnp.where((rank < n_top) & (blk_lane < i), 1.0, 0.0).astype(BF16)
        selk = _dot(sel, expand)
        mask = (selk > 0.5) | own_mask
        s = _dot_nt(q, k.astype(BF16)) * scale
        outs.append(_softmax_pv(s, mask, v))
    o_ref[...] = jnp.concatenate(outs, axis=1)


def _moba_attention(qb, kb, vb, batch, s_len):
    n = qb.shape[0]
    nb = s_len // MOBA_BLOCK
    n_top = min(MOBA_TOPK, nb - 1)
    hp = B_HEADS * B_HEAD_DIM // LANES
    kern = functools.partial(_moba_kernel, n_top=n_top, s_len=s_len)
    return pl.pallas_call(
        kern, grid=(batch, hp, nb),
        in_specs=[pl.BlockSpec((MOBA_BLOCK, LANES), lambda b, g, i: (b * nb + i, g)),
                  pl.BlockSpec((s_len, LANES), lambda b, g, i: (b, g)),
                  pl.BlockSpec((s_len, LANES), lambda b, g, i: (b, g))],
        out_specs=pl.BlockSpec((MOBA_BLOCK, LANES), lambda b, g, i: (b * nb + i, g)),
        out_shape=jax.ShapeDtypeStruct((n, B_HEADS * B_HEAD_DIM), F32),
        compiler_params=_params("parallel", "parallel", "arbitrary"), name="moba_attention",
    )(qb, kb, vb)


MLA_GROUP = 4


def _mla_kernel(qn_ref, qr_ref, kn_ref, kr_ref, v_ref, o_ref, *, tq, s_len):
    i = pl.program_id(2)
    scale = (C_NOPE + C_ROPE) ** -0.5
    key_pos = lax.broadcasted_iota(I32, (tq, s_len), 1)
    q_pos = i * tq + lax.broadcasted_iota(I32, (tq, s_len), 0)
    causal = key_pos <= q_pos
    kr = kr_ref[:, 0:C_ROPE].astype(BF16)
    outs = []
    for hh in range(MLA_GROUP):
        qn = qn_ref[:, hh * C_NOPE:(hh + 1) * C_NOPE].astype(BF16)
        qr = qr_ref[:, hh * C_ROPE:(hh + 1) * C_ROPE].astype(BF16)
        kn = kn_ref[:, hh * C_NOPE:(hh + 1) * C_NOPE].astype(BF16)
        v = v_ref[:, hh * C_V:(hh + 1) * C_V].astype(BF16)
        s = (_dot_nt(qn, kn) + _dot_nt(qr, kr)) * scale
        outs.append(_softmax_pv(s, causal, v))
    o_ref[...] = jnp.concatenate(outs, axis=1)


def _mla_attention(qn, qr, kn, kr, v, batch, s_len, tq=256):
    n = qn.shape[0]
    nq = s_len // tq
    ng = C_HEADS // MLA_GROUP
    kern = functools.partial(_mla_kernel, tq=tq, s_len=s_len)
    return pl.pallas_call(
        kern, grid=(batch, ng, nq),
        in_specs=[pl.BlockSpec((tq, MLA_GROUP * C_NOPE), lambda b, g, i: (b * nq + i, g)),
                  pl.BlockSpec((tq, MLA_GROUP * C_ROPE), lambda b, g, i: (b * nq + i, g)),
                  pl.BlockSpec((s_len, MLA_GROUP * C_NOPE), lambda b, g, i: (b, g)),
                  pl.BlockSpec((s_len, LANES), lambda b, g, i: (b, 0)),
                  pl.BlockSpec((s_len, MLA_GROUP * C_V), lambda b, g, i: (b, g))],
        out_specs=pl.BlockSpec((tq, MLA_GROUP * C_V), lambda b, g, i: (b * nq + i, g)),
        out_shape=jax.ShapeDtypeStruct((n, C_HEADS * C_V), F32),
        compiler_params=_params("parallel", "parallel", "arbitrary"), name="mla_attention",
    )(qn, qr, kn, kr, v)


def _topk_rows(sc, k):
    r = sc.shape[0]
    riota = lax.broadcasted_iota(I32, sc.shape, 0)
    vals, rows = [], []
    for _ in range(k):
        m = jnp.max(sc, axis=0, keepdims=True)
        pos = jnp.min(jnp.where(sc == m, riota, r), axis=0, keepdims=True)
        sc = jnp.where(riota == pos, -jnp.inf, sc)
        vals.append(m)
        rows.append(pos)
    return jnp.concatenate(vals, axis=0), jnp.concatenate(rows, axis=0)


def _peer_topk_kernel(q_ref, keys_ref, e_ref, g_ref):
    t = q_ref.shape[0]
    half_v, half_i = [], []
    for p in range(2):
        qh = q_ref[:, p * PEER_NKEYS:(p + 1) * PEER_NKEYS].astype(BF16)
        sc = _dot_nt(keys_ref[p], qh)
        v, ix = _topk_rows(sc, PEER_TOPK_HALF)
        half_v.append(v)
        half_i.append(ix)
    kh = PEER_TOPK_HALF
    cand = (half_v[0][:, None, :] + half_v[1][None, :, :]).reshape(kh * kh, t)
    cand_idx = (half_i[0][:, None, :] * PEER_NKEYS + half_i[1][None, :, :]).reshape(kh * kh, t)
    top_s, pos = _topk_rows(cand, PEER_TOPK)
    piota = lax.broadcasted_iota(I32, (kh * kh, t), 0)
    experts = [jnp.max(jnp.where(piota == pos[j:j + 1, :], cand_idx, -1), axis=0, keepdims=True)
               for j in range(PEER_TOPK)]
    e_ref[...] = jnp.concatenate(experts, axis=0)
    ex = jnp.exp(top_s - top_s[0:1, :])
    g_ref[...] = ex / jnp.sum(ex, axis=0, keepdims=True)


def _peer_topk(q, keys_bf, t=128):
    n = q.shape[0]
    nt = n // t
    rows = PEER_HEADS * PEER_TOPK
    return pl.pallas_call(
        _peer_topk_kernel, grid=(nt, PEER_HEADS),
        in_specs=[pl.BlockSpec((t, PEER_QDIM), lambda i, h: (i, h)),
                  pl.BlockSpec((None, 2, PEER_NKEYS, PEER_QDIM // 2), lambda i, h: (h, 0, 0, 0))],
        out_specs=[pl.BlockSpec((None, PEER_TOPK, t), lambda i, h: (i, h, 0))] * 2,
        out_shape=[jax.ShapeDtypeStruct((nt, rows, t), I32), jax.ShapeDtypeStruct((nt, rows, t), F32)],
        compiler_params=_params("parallel", "arbitrary"), name="peer_topk",
    )(q, keys_bf)


PEER_SEL = PEER_HEADS * PEER_TOPK

SC_CORES = 2
SC_SUBCORES = 16
SC_LANES = 16
SC_WORKERS = SC_CORES * SC_SUBCORES
SC_ROWS = 32
SC_TOK = 8
SC_CHUNKS = SC_TOK * PEER_SEL // SC_ROWS
SC_VCOLS = 256
SC_UNROLL = 4


def _sc_mesh():
    return plsc.VectorSubcoreMesh(core_axis_name="c", subcore_axis_name="s",
                                  num_cores=SC_CORES, num_subcores=SC_SUBCORES)


def _sc_chunk_pipeline(gather, compute):
    gather(0, 0).start()

    @pl.loop(0, SC_CHUNKS, step=2)
    def _(q):
        gather(q + 1, 1).start()
        gather(q, 0).wait()
        compute(q, 0)

        @pl.when(q + 2 < SC_CHUNKS)
        def _():
            gather(q + 2, 0).start()

        gather(q + 1, 1).wait()
        compute(q + 1, 1)


def _sc_u_body(idx_hbm, h_hbm, u_hbm, act_hbm, idx_v, h_v, rows_v, act_v, sem0, sem1, *, tpw, d):
    tok_base = (lax.axis_index("s") * SC_CORES + lax.axis_index("c")) * tpw
    sems = (sem0, sem1)

    def gather(q, b):
        return pltpu.make_async_copy(u_hbm.at[idx_v.at[pl.ds(q * SC_ROWS, SC_ROWS)]], rows_v.at[b], sems[b])

    def compute(q, b):
        hoff = (q // (PEER_SEL // SC_ROWS)) * d
        lane = lax.iota(I32, SC_LANES)
        for g in range(SC_ROWS // SC_LANES):
            def body(c, accs, g=g):
                xv = h_v[pl.ds(hoff + c * SC_LANES, SC_LANES)]
                return tuple(a + rows_v[b, g * SC_LANES + r, pl.ds(c * SC_LANES, SC_LANES)] * xv
                             for r, a in enumerate(accs))
            accs = lax.fori_loop(0, d // SC_LANES, body,
                                 tuple(jnp.zeros((SC_LANES,), F32) for _ in range(SC_LANES)),
                                 unroll=SC_UNROLL)
            out = jnp.zeros((SC_LANES,), F32)
            for r in range(SC_LANES):
                out = jnp.where(lane == r, jnp.sum(accs[r]), out)
            act_v[pl.ds(q * SC_ROWS + g * SC_LANES, SC_LANES)] = out

    @pl.loop(0, tpw // SC_TOK)
    def _(blk):
        tok0 = tok_base + blk * SC_TOK
        pltpu.sync_copy(idx_hbm.at[pl.ds(tok0 * PEER_SEL, SC_TOK * PEER_SEL)], idx_v)
        pltpu.sync_copy(h_hbm.at[pl.ds(tok0 * d, SC_TOK * d)], h_v)
        _sc_chunk_pipeline(gather, compute)
        pltpu.sync_copy(act_v, act_hbm.at[pl.ds(tok0 * PEER_SEL, SC_TOK * PEER_SEL)])


def _sc_peer_u(idx_flat, h_flat, u_tab):
    d = u_tab.shape[1]
    n = h_flat.shape[0] // d
    assert n % (SC_WORKERS * SC_TOK) == 0 and d % SC_VCOLS == 0
    body = functools.partial(_sc_u_body, tpw=n // SC_WORKERS, d=d)
    return pl.kernel(
        body,
        out_type=jax.ShapeDtypeStruct((n * PEER_SEL,), F32),
        mesh=_sc_mesh(),
        scratch_types=[pltpu.VMEM((SC_TOK * PEER_SEL,), I32),
                       pltpu.VMEM((SC_TOK * d,), F32),
                       pltpu.VMEM((2, SC_ROWS, d), F32),
                       pltpu.VMEM((SC_TOK * PEER_SEL,), F32),
                       pltpu.SemaphoreType.DMA, pltpu.SemaphoreType.DMA],
        compiler_params=pltpu.CompilerParams(needs_layout_passes=False),
        name="sc_peer_u",
    )(idx_flat, h_flat, u_tab)


def _sc_v_body(idx_hbm, w_hbm, x_hbm, v_hbm, o_hbm, idx_v, w_v, out_v, rows_v, sem0, sem1, *, tpw, d):
    tok_base = (lax.axis_index("s") * SC_CORES + lax.axis_index("c")) * tpw
    sems = (sem0, sem1)
    nacc = SC_VCOLS // SC_LANES

    def gather(q, b):
        return pltpu.make_async_copy(v_hbm.at[idx_v.at[pl.ds(q * SC_ROWS, SC_ROWS)]], rows_v.at[b], sems[b])

    def compute(q, b):
        ooff = (q // (PEER_SEL // SC_ROWS)) * d
        for cp in range(d // SC_VCOLS):
            col0 = cp * SC_VCOLS

            def body(r, accs, col0=col0):
                wv = plsc.load_gather(w_v, [jnp.full((SC_LANES,), q * SC_ROWS + r, I32)])
                return tuple(acc + wv * rows_v[b, r, pl.ds(col0 + a * SC_LANES, SC_LANES)]
                             for a, acc in enumerate(accs))

            accs = lax.fori_loop(
                0, SC_ROWS, body,
                tuple(out_v[pl.ds(ooff + col0 + a * SC_LANES, SC_LANES)] for a in range(nacc)))
            for a in range(nacc):
                out_v[pl.ds(ooff + col0 + a * SC_LANES, SC_LANES)] = accs[a]

    @pl.loop(0, tpw // SC_TOK)
    def _(blk):
        tok0 = tok_base + blk * SC_TOK
        pltpu.sync_copy(idx_hbm.at[pl.ds(tok0 * PEER_SEL, SC_TOK * PEER_SEL)], idx_v)
        pltpu.sync_copy(w_hbm.at[pl.ds(tok0 * PEER_SEL, SC_TOK * PEER_SEL)], w_v)
        pltpu.sync_copy(x_hbm.at[pl.ds(tok0 * d, SC_TOK * d)], out_v)
        _sc_chunk_pipeline(gather, compute)
        pltpu.sync_copy(out_v, o_hbm.at[pl.ds(tok0 * d, SC_TOK * d)])


def _sc_peer_v(idx_flat, w_flat, x_flat, v_tab):
    d = v_tab.shape[1]
    n = x_flat.shape[0] // d
    assert n % (SC_WORKERS * SC_TOK) == 0 and d % SC_VCOLS == 0
    body = functools.partial(_sc_v_body, tpw=n // SC_WORKERS, d=d)
    return pl.kernel(
        body,
        out_type=jax.ShapeDtypeStruct((n * d,), F32),
        mesh=_sc_mesh(),
        scratch_types=[pltpu.VMEM((SC_TOK * PEER_SEL,), I32),
                       pltpu.VMEM((SC_TOK * PEER_SEL,), F32),
                       pltpu.VMEM((SC_TOK * d,), F32),
                       pltpu.VMEM((2, SC_ROWS, d), F32),
                       pltpu.SemaphoreType.DMA, pltpu.SemaphoreType.DMA],
        compiler_params=pltpu.CompilerParams(needs_layout_passes=False),
        name="sc_peer_v",
    )(idx_flat, w_flat, x_flat, v_tab)


def _peer_gate_kernel(g_ref, act_ref, w_ref):
    act = act_ref[...]
    w_ref[...] = g_ref[...] * (0.5 * act * (1.0 + lax.erf(act * (2.0 ** -0.5))))


def _peer_gate(g, act, tm=2048):
    n = g.shape[0]
    spec = pl.BlockSpec((tm, PEER_SEL), lambda i: (i, 0))
    return pl.pallas_call(
        _peer_gate_kernel, grid=(n // tm,), in_specs=[spec, spec], out_specs=spec,
        out_shape=jax.ShapeDtypeStruct((n, PEER_SEL), F32),
        compiler_params=_params("parallel"), name="peer_gate",
    )(g, act)


def _pad_cols(w, width):
    return jnp.pad(w, ((0, 0), (0, width - w.shape[1])))


def _even_weights(w_in):
    hd = A_HEADS * A_HEAD_DIM
    cols = np.cumsum([0, hd, A_HEAD_DIM, A_HEAD_DIM, IDX_HEADS * IDX_DIM, IDX_DIM, IDX_HEADS,
                      B_HEADS * B_HEAD_DIM, B_HEADS * B_HEAD_DIM, B_HEADS * B_HEAD_DIM])
    qa, ka, va, iq, ik, iw, qb, kb, vb = [w_in[:, cols[j]:cols[j + 1]] for j in range(9)]
    vaiw = _pad_cols(jnp.concatenate([va, iw], axis=1), LANES)
    w = jnp.concatenate([qa, iq, qb, kb, vb, ka, ik, vaiw], axis=1).astype(BF16)
    groups = [(0, 512, "r64"), (512, 512, "r64"), (1024, 512, "r64"), (1536, 512, "r64"),
              (2048, 512, None), (2560, LANES, "r64"), (2560 + LANES, LANES, None)]
    return w, groups


def _even_mixer(x, gain, w_in, w_out, tables, batch, s_len):
    w, groups = _even_weights(w_in)
    qa, iq, qb, kb, vb, kaik, vaiw = _fused_proj(x, gain, w, groups, tables)
    out_a = _dsa_attention(iq, qa, kaik, vaiw, batch, s_len)
    out_b = _moba_attention(qb, kb, vb, batch, s_len)
    hd = A_HEADS * A_HEAD_DIM
    w_out = w_out.astype(BF16)
    return _outproj_residual(x, [out_a, out_b], [w_out[:hd], w_out[hd:]])


def _mla_mixer(x, gain, w_in, q_norm, kv_norm, w_uq, w_ukv, w_out, tables, batch, s_len):
    w1 = _pad_cols(w_in, C_Q_RANK + C_KV_RANK + LANES).astype(BF16)
    groups1 = [(0, C_Q_RANK, None), (C_Q_RANK, C_KV_RANK, None), (C_Q_RANK + C_KV_RANK, LANES, "r32")]
    cq, ckv, kr = _fused_proj(x, gain, w1, groups1, tables)
    qd = C_NOPE + C_ROPE
    nope_cols = np.concatenate([np.arange(h * qd, h * qd + C_NOPE) for h in range(C_HEADS)])
    rope_cols = np.concatenate([np.arange(h * qd + C_NOPE, (h + 1) * qd) for h in range(C_HEADS)])
    w2 = jnp.concatenate([w_uq[:, nope_cols], w_uq[:, rope_cols]], axis=1).astype(BF16)
    groups2 = [(0, C_HEADS * C_NOPE, None), (C_HEADS * C_NOPE, C_HEADS * C_ROPE, "r32")]
    qn, qr = _fused_proj(cq, q_norm, w2, groups2, tables)
    kd = C_NOPE + C_V
    kn_cols = np.concatenate([np.arange(h * kd, h * kd + C_NOPE) for h in range(C_HEADS)])
    v_cols = np.concatenate([np.arange(h * kd + C_NOPE, (h + 1) * kd) for h in range(C_HEADS)])
    w3 = jnp.concatenate([w_ukv[:, kn_cols], w_ukv[:, v_cols]], axis=1).astype(BF16)
    groups3 = [(0, C_HEADS * C_NOPE, None), (C_HEADS * C_NOPE, C_HEADS * C_V, None)]
    kn, v = _fused_proj(ckv, kv_norm, w3, groups3, tables)
    o = _mla_attention(qn, qr, kn, kr, v, batch, s_len)
    return _outproj_residual(x, [o], [w_out.astype(BF16)])


def _peer_ffn(x, gain, w_q, sub_keys, u_tab, v_tab):
    n, d = x.shape
    q, h = _fused_proj(x, gain, w_q.astype(BF16), [(0, PEER_HEADS * PEER_QDIM, None)], {}, emit_h=True)
    experts, gates = _peer_topk(q, sub_keys.astype(BF16))
    idx = jnp.swapaxes(experts, 1, 2).reshape(n, PEER_SEL)
    g = jnp.swapaxes(gates, 1, 2).reshape(n, PEER_SEL)
    idx_flat = idx.reshape(n * PEER_SEL)
    act = _sc_peer_u(idx_flat, h.reshape(n * d), u_tab)
    w = _peer_gate(g, act.reshape(n, PEER_SEL))
    return _sc_peer_v(idx_flat, w.reshape(n * PEER_SEL), x.reshape(n * d), v_tab).reshape(n, d)


def kernel(x, positions, attn_norm, ffn_norm, final_norm, hyb_w_in, hyb_w_out, mla_w_in, mla_q_norm,
           mla_kv_norm, mla_w_uq, mla_w_ukv, mla_w_out, peer_w_q, peer_sub_keys, peer_u, peer_v):
    batch, s_len, d = x.shape
    depth = attn_norm.shape[0]
    gb = batch // BATCH_GROUPS
    gn = gb * s_len
    xs, tabs = [], []
    for g in range(BATCH_GROUPS):
        pos = positions[g * gb:(g + 1) * gb].reshape(gn, 1).astype(I32)
        tabs.append({"r64": _rope_tables(pos, A_HEAD_DIM), "r32": _rope_tables(pos, C_ROPE)})
        xs.append(x[g * gb:(g + 1) * gb].reshape(gn, d))
    for i in range(depth):
        j = i // 2
        for g in range(BATCH_GROUPS):
            if i % 2 == 0:
                xs[g] = _even_mixer(xs[g], attn_norm[i], hyb_w_in[j], hyb_w_out[j], tabs[g], gb, s_len)
            else:
                xs[g] = _mla_mixer(xs[g], attn_norm[i], mla_w_in[j], mla_q_norm[j], mla_kv_norm[j],
                                   mla_w_uq[j], mla_w_ukv[j], mla_w_out[j], tabs[g], gb, s_len)
            xs[g] = _peer_ffn(xs[g], ffn_norm[i], peer_w_q[i], peer_sub_keys[i], peer_u[i], peer_v[i])
    outs = [_rmsnorm(xg, final_norm).reshape(gb, s_len, d) for xg in xs]
    return jnp.concatenate(outs, axis=0)
```

```python
import functools

import numpy as np
import jax
import jax.numpy as jnp
from jax import lax
from jax.experimental import pallas as pl
from jax.experimental.pallas import tpu as pltpu
from jax.experimental.pallas import tpu_sc as plsc

NORM_EPS = 1e-6
ROPE_THETA = 10000.0

A_HEADS = 8
A_HEAD_DIM = 64
IDX_HEADS = 8
IDX_DIM = 64
DSA_TOPK_MAX = 256

B_HEADS = 8
B_HEAD_DIM = 64
MOBA_BLOCK = 256
MOBA_TOPK = 3

C_HEADS = 16
C_NOPE = 64
C_ROPE = 32
C_V = 64
C_Q_RANK = 768
C_KV_RANK = 256

PEER_HEADS = 8
PEER_NKEYS = 128
PEER_QDIM = 256
PEER_TOPK_HALF = 16
PEER_TOPK = 16

BATCH_GROUPS = 2
LANES = 128
VMEM_LIMIT = 56 * 1024 * 1024

F32 = jnp.float32
BF16 = jnp.bfloat16
I32 = jnp.int32
INT_MIN = -(2 ** 31)


def _params(*sem):
    return pltpu.CompilerParams(dimension_semantics=sem, vmem_limit_bytes=VMEM_LIMIT)


def _dot(a, b):
    return jnp.dot(a, b, preferred_element_type=F32)


def _dot_nt(a, b):
    return lax.dot_general(a, b, (((1,), (1,)), ((), ())), preferred_element_type=F32)


def _rope_table_kernel(pos_ref, inv_ref, sign_ref, cos_ref, sin_ref):
    ang = pos_ref[...].astype(F32) * inv_ref[...]
    cos_ref[...] = jnp.cos(ang)
    sin_ref[...] = jnp.sin(ang) * sign_ref[...]


def _rope_tables(pos, dim, tm=1024):
    n = pos.shape[0]
    half = dim // 2
    inv = 1.0 / (ROPE_THETA ** (jnp.arange(0, dim, 2, dtype=F32) / dim))
    reps = LANES // dim
    inv_l = jnp.tile(jnp.concatenate([inv, inv]), reps)[None, :]
    sign_l = jnp.tile(jnp.concatenate([-jnp.ones(half, F32), jnp.ones(half, F32)]), reps)[None, :]
    out = jax.ShapeDtypeStruct((n, LANES), F32)
    return pl.pallas_call(
        _rope_table_kernel,
        grid=(n // tm,),
        in_specs=[pl.BlockSpec((tm, 1), lambda i: (i, 0)),
                  pl.BlockSpec((1, LANES), lambda i: (0, 0)),
                  pl.BlockSpec((1, LANES), lambda i: (0, 0))],
        out_specs=[pl.BlockSpec((tm, LANES), lambda i: (i, 0))] * 2,
        out_shape=[out, out],
        compiler_params=_params("parallel"),
        name="rope_tables",
    )(pos, inv_l, sign_l)


def _rope_epilogue(acc, cos, sin, half):
    wd = acc.shape[1]
    reps = wd // LANES
    if reps > 1:
        cos = jnp.concatenate([cos] * reps, axis=1)
        sin = jnp.concatenate([sin] * reps, axis=1)
    lane = lax.broadcasted_iota(I32, acc.shape, 1)
    first = (lane % (2 * half)) < half
    partner = jnp.where(first, pltpu.roll(acc, wd - half, 1), pltpu.roll(acc, half, 1))
    return acc * cos + partner * sin


def _proj_kernel(*refs, groups, has_gain, emit_h, tab_names):
    it = iter(refs)
    x_ref = next(it)
    g_ref = next(it) if has_gain else None
    w_ref = next(it)
    tabs = {name: (next(it), next(it)) for name in tab_names}
    out_refs = [next(it) for _ in groups]
    h_ref = next(it) if emit_h else None

    x = x_ref[...]
    if has_gain:
        ms = jnp.mean(x * x, axis=-1, keepdims=True)
        h = x * lax.rsqrt(ms + NORM_EPS) * g_ref[...]
    else:
        h = x
    if emit_h:
        h_ref[...] = h
    hb = h.astype(BF16)
    for o_ref, (c0, wd, rope) in zip(out_refs, groups):
        acc = _dot(hb, w_ref[:, c0:c0 + wd])
        if rope is not None:
            cos_ref, sin_ref = tabs[rope]
            acc = _rope_epilogue(acc, cos_ref[...], sin_ref[...], 32 if rope == "r64" else 16)
        o_ref[...] = acc


def _fused_proj(x, gain, w, groups, tables, emit_h=False, tm=256):
    n, k = x.shape
    tab_names = sorted({g[2] for g in groups if g[2] is not None})
    args = [x]
    in_specs = [pl.BlockSpec((tm, k), lambda i: (i, 0))]
    if gain is not None:
        args.append(gain.reshape(1, k).astype(F32))
        in_specs.append(pl.BlockSpec((1, k), lambda i: (0, 0)))
    args.append(w)
    in_specs.append(pl.BlockSpec(w.shape, lambda i: (0, 0)))
    for name in tab_names:
        for t in tables[name]:
            args.append(t)
            in_specs.append(pl.BlockSpec((tm, LANES), lambda i: (i, 0)))
    out_shape = [jax.ShapeDtypeStruct((n, wd), F32) for (_, wd, _) in groups]
    out_specs = [pl.BlockSpec((tm, wd), lambda i: (i, 0)) for (_, wd, _) in groups]
    if emit_h:
        out_shape.append(jax.ShapeDtypeStruct((n, k), F32))
        out_specs.append(pl.BlockSpec((tm, k), lambda i: (i, 0)))
    kern = functools.partial(_proj_kernel, groups=tuple(groups), has_gain=gain is not None,
                             emit_h=emit_h, tab_names=tuple(tab_names))
    return pl.pallas_call(
        kern, grid=(n // tm,), in_specs=in_specs, out_specs=out_specs, out_shape=out_shape,
        compiler_params=_params("parallel"), name="fused_proj",
    )(*args)


def _outproj_kernel(*refs, n_in):
    res_ref = refs[0]
    a_refs = refs[1:1 + n_in]
    w_refs = refs[1 + n_in:1 + 2 * n_in]
    o_ref = refs[-1]
    acc = res_ref[...]
    for a_ref, w_ref in zip(a_refs, w_refs):
        acc = acc + _dot(a_ref[...].astype(BF16), w_ref[...])
    o_ref[...] = acc


def _outproj_residual(res, acts, ws, tm=256):
    n, d = res.shape
    in_specs = [pl.BlockSpec((tm, d), lambda i: (i, 0))]
    in_specs += [pl.BlockSpec((tm, a.shape[1]), lambda i: (i, 0)) for a in acts]
    in_specs += [pl.BlockSpec(w.shape, lambda i: (0, 0)) for w in ws]
    return pl.pallas_call(
        functools.partial(_outproj_kernel, n_in=len(acts)),
        grid=(n // tm,), in_specs=in_specs,
        out_specs=pl.BlockSpec((tm, d), lambda i: (i, 0)),
        out_shape=jax.ShapeDtypeStruct((n, d), F32),
        compiler_params=_params("parallel"), name="outproj_residual",
    )(res, *acts, *ws)


def _rmsnorm_kernel(x_ref, g_ref, o_ref):
    x = x_ref[...]
    ms = jnp.mean(x * x, axis=-1, keepdims=True)
    o_ref[...] = x * lax.rsqrt(ms + NORM_EPS) * g_ref[...]


def _rmsnorm(x, gain, tm=512):
    n, d = x.shape
    return pl.pallas_call(
        _rmsnorm_kernel, grid=(n // tm,),
        in_specs=[pl.BlockSpec((tm, d), lambda i: (i, 0)), pl.BlockSpec((1, d), lambda i: (0, 0))],
        out_specs=pl.BlockSpec((tm, d), lambda i: (i, 0)),
        out_shape=jax.ShapeDtypeStruct((n, d), F32),
        compiler_params=_params("parallel"), name="final_rmsnorm",
    )(x, gain.reshape(1, d))


def _softmax_pv(s, mask, v_bf):
    s = jnp.where(mask, s, -jnp.inf)
    m = jnp.max(s, axis=-1, keepdims=True)
    p = jnp.exp(s - m)
    l = jnp.sum(p, axis=-1, keepdims=True)
    return _dot(p.astype(BF16), v_bf) / l


def _dsa_kernel(iq_ref, qa_ref, kaik_ref, va_ref, iw_ref, o_ref, *, n_sel, tq, s_len):
    i = pl.program_id(1)
    kaik = kaik_ref[...]
    ka = kaik[:, 0:A_HEAD_DIM].astype(BF16)
    ik = kaik[:, A_HEAD_DIM:A_HEAD_DIM + IDX_DIM].astype(BF16)
    va = va_ref[:, 0:A_HEAD_DIM].astype(BF16)
    iw = iw_ref[:, A_HEAD_DIM:A_HEAD_DIM + IDX_HEADS]
    iq = iq_ref[...]

    score = jnp.zeros((tq, s_len), F32)
    for h in range(IDX_HEADS):
        lg = _dot_nt(iq[:, h * IDX_DIM:(h + 1) * IDX_DIM].astype(BF16), ik)
        score = score + jnp.maximum(lg, 0.0) * iw[:, h:h + 1]
    w_scale = (IDX_HEADS ** -0.5) * (IDX_DIM ** -0.5)
    score = score * w_scale + 0.0

    key_pos = lax.broadcasted_iota(I32, (tq, s_len), 1)
    q_pos = i * tq + lax.broadcasted_iota(I32, (tq, s_len), 0)
    causal = key_pos <= q_pos

    bits = pltpu.bitcast(score, I32)
    key = jnp.where(bits < 0, bits ^ 0x7FFFFFFF, bits)
    key = jnp.where(causal, key, INT_MIN)

    def count_ge(cand):
        return jnp.sum(jnp.where(key >= cand, 1.0, 0.0), axis=-1, keepdims=True)

    lo = jnp.where(count_ge(jnp.zeros((tq, 1), I32)) >= n_sel, 0, INT_MIN).astype(I32)

    def bisect(t, lo):
        cand = lo + jnp.left_shift(jnp.int32(1), 30 - t)
        return jnp.where(count_ge(cand) >= n_sel, cand, lo)

    thr = lax.fori_loop(0, 31, bisect, lo)

    gt = key > thr
    eq = key == thr
    need = n_sel - jnp.sum(jnp.where(gt, 1.0, 0.0), axis=-1, keepdims=True)
    r = lax.broadcasted_iota(I32, (LANES, LANES), 0)
    c = lax.broadcasted_iota(I32, (LANES, LANES), 1)
    tri = jnp.where(r < c, 1.0, 0.0).astype(BF16)
    eq_bf = jnp.where(eq, 1.0, 0.0).astype(BF16)
    carry = jnp.zeros((tq, 1), F32)
    pref = []
    for cb in range(s_len // LANES):
        blk = eq_bf[:, cb * LANES:(cb + 1) * LANES]
        pref.append(_dot(blk, tri) + carry)
        carry = carry + jnp.sum(blk.astype(F32), axis=-1, keepdims=True)
    prefix = jnp.concatenate(pref, axis=1)
    sel = (gt | (eq & (prefix < need))) & causal

    a_scale = A_HEAD_DIM ** -0.5
    qa = qa_ref[...]
    outs = []
    for h in range(A_HEADS):
        sc = _dot_nt(qa[:, h * A_HEAD_DIM:(h + 1) * A_HEAD_DIM].astype(BF16), ka) * a_scale
        outs.append(_softmax_pv(sc, sel, va))
    o_ref[...] = jnp.concatenate(outs, axis=1)


def _dsa_attention(iq, qa, kaik, vaiw, batch, s_len, tq=128):
    n = iq.shape[0]
    nq = s_len // tq
    n_sel = min(DSA_TOPK_MAX, s_len // 4)
    hd = A_HEADS * A_HEAD_DIM
    kern = functools.partial(_dsa_kernel, n_sel=n_sel, tq=tq, s_len=s_len)
    return pl.pallas_call(
        kern, grid=(batch, nq),
        in_specs=[pl.BlockSpec((tq, IDX_HEADS * IDX_DIM), lambda b, i: (b * nq + i, 0)),
                  pl.BlockSpec((tq, hd), lambda b, i: (b * nq + i, 0)),
                  pl.BlockSpec((s_len, LANES), lambda b, i: (b, 0)),
                  pl.BlockSpec((s_len, LANES), lambda b, i: (b, 0)),
                  pl.BlockSpec((tq, LANES), lambda b, i: (b * nq + i, 0))],
        out_specs=pl.BlockSpec((tq, hd), lambda b, i: (b * nq + i, 0)),
        out_shape=jax.ShapeDtypeStruct((n, hd), F32),
        compiler_params=_params("parallel", "arbitrary"), name="dsa_attention",
    )(iq, qa, kaik, vaiw, vaiw)


def _moba_kernel(q_ref, k_ref, v_ref, o_ref, *, n_top, s_len):
    i = pl.program_id(2)
    tq = MOBA_BLOCK
    nb = s_len // MOBA_BLOCK
    d = B_HEAD_DIM
    scale = d ** -0.5
    key_pos = lax.broadcasted_iota(I32, (tq, s_len), 1)
    q_pos = i * tq + lax.broadcasted_iota(I32, (tq, s_len), 0)
    own_mask = (key_pos >= i * MOBA_BLOCK) & (key_pos <= q_pos)
    blk_lane = lax.broadcasted_iota(I32, (tq, nb), 1)
    expand = jnp.where(lax.broadcasted_iota(I32, (nb, s_len), 1) // MOBA_BLOCK
                       == lax.broadcasted_iota(I32, (nb, s_len), 0), 1.0, 0.0).astype(BF16)
    outs = []
    for hh in range(LANES // d):
        q = q_ref[:, hh * d:(hh + 1) * d].astype(BF16)
        k = k_ref[:, hh * d:(hh + 1) * d]
        v = v_ref[:, hh * d:(hh + 1) * d].astype(BF16)
        k_mean = jnp.sum(k.reshape(nb, MOBA_BLOCK, d), axis=1) / MOBA_BLOCK
        gate = _dot_nt(q, k_mean.astype(BF16))
        gate = jnp.where(blk_lane < i, gate, -jnp.inf)
        rank = jnp.zeros((tq, nb), F32)
        for b2 in range(nb):
            gb = gate[:, b2:b2 + 1]
            beats = (gb > gate) | ((gb == gate) & (b2 < blk_lane))
            rank = rank + jnp.where(beats, 1.0, 0.0)
        sel = jnp.where((rank < n_top) & (blk_lane < i), 1.0, 0.0).astype(BF16)
        selk = _dot(sel, expand)
        mask = (selk > 0.5) | own_mask
        s = _dot_nt(q, k.astype(BF16)) * scale
        outs.append(_softmax_pv(s, mask, v))
    o_ref[...] = jnp.concatenate(outs, axis=1)


def _moba_attention(qb, kb, vb, batch, s_len):
    n = qb.shape[0]
    nb = s_len // MOBA_BLOCK
    n_top = min(MOBA_TOPK, nb - 1)
    hp = B_HEADS * B_HEAD_DIM // LANES
    kern = functools.partial(_moba_kernel, n_top=n_top, s_len=s_len)
    return pl.pallas_call(
        kern, grid=(batch, hp, nb),
        in_specs=[pl.BlockSpec((MOBA_BLOCK, LANES), lambda b, g, i: (b * nb + i, g)),
                  pl.BlockSpec((s_len, LANES), lambda b, g, i: (b, g)),
                  pl.BlockSpec((s_len, LANES), lambda b, g, i: (b, g))],
        out_specs=pl.BlockSpec((MOBA_BLOCK, LANES), lambda b, g, i: (b * nb + i, g)),
        out_shape=jax.ShapeDtypeStruct((n, B_HEADS * B_HEAD_DIM), F32),
        compiler_params=_params("parallel", "parallel", "arbitrary"), name="moba_attention",
    )(qb, kb, vb)


MLA_GROUP = 4


def _mla_kernel(qn_ref, qr_ref, kn_ref, kr_ref, v_ref, o_ref, *, tq, s_len):
    i = pl.program_id(2)
    scale = (C_NOPE + C_ROPE) ** -0.5
    key_pos = lax.broadcasted_iota(I32, (tq, s_len), 1)
    q_pos = i * tq + lax.broadcasted_iota(I32, (tq, s_len), 0)
    causal = key_pos <= q_pos
    kr = kr_ref[:, 0:C_ROPE].astype(BF16)
    outs = []
    for hh in range(MLA_GROUP):
        qn = qn_ref[:, hh * C_NOPE:(hh + 1) * C_NOPE].astype(BF16)
        qr = qr_ref[:, hh * C_ROPE:(hh + 1) * C_ROPE].astype(BF16)
        kn = kn_ref[:, hh * C_NOPE:(hh + 1) * C_NOPE].astype(BF16)
        v = v_ref[:, hh * C_V:(hh + 1) * C_V].astype(BF16)
        s = (_dot_nt(qn, kn) + _dot_nt(qr, kr)) * scale
        outs.append(_softmax_pv(s, causal, v))
    o_ref[...] = jnp.concatenate(outs, axis=1)


def _mla_attention(qn, qr, kn, kr, v, batch, s_len, tq=256):
    n = qn.shape[0]
    nq = s_len // tq
    ng = C_HEADS // MLA_GROUP
    kern = functools.partial(_mla_kernel, tq=tq, s_len=s_len)
    return pl.pallas_call(
        kern, grid=(batch, ng, nq),
        in_specs=[pl.BlockSpec((tq, MLA_GROUP * C_NOPE), lambda b, g, i: (b * nq + i, g)),
                  pl.BlockSpec((tq, MLA_GROUP * C_ROPE), lambda b, g, i: (b * nq + i, g)),
                  pl.BlockSpec((s_len, MLA_GROUP * C_NOPE), lambda b, g, i: (b, g)),
                  pl.BlockSpec((s_len, LANES), lambda b, g, i: (b, 0)),
                  pl.BlockSpec((s_len, MLA_GROUP * C_V), lambda b, g, i: (b, g))],
        out_specs=pl.BlockSpec((tq, MLA_GROUP * C_V), lambda b, g, i: (b * nq + i, g)),
        out_shape=jax.ShapeDtypeStruct((n, C_HEADS * C_V), F32),
        compiler_params=_params("parallel", "parallel", "arbitrary"), name="mla_attention",
    )(qn, qr, kn, kr, v)


def _topk_rows(sc, k):
    r = sc.shape[0]
    riota = lax.broadcasted_iota(I32, sc.shape, 0)
    vals, rows = [], []
    for _ in range(k):
        m = jnp.max(sc, axis=0, keepdims=True)
        pos = jnp.min(jnp.where(sc == m, riota, r), axis=0, keepdims=True)
        sc = jnp.where(riota == pos, -jnp.inf, sc)
        vals.append(m)
        rows.append(pos)
    return jnp.concatenate(vals, axis=0), jnp.concatenate(rows, axis=0)


def _peer_candidates():
    k = PEER_TOPK_HALF
    rows = [(0, b) for b in range(k)]
    rows += [(a, b) for a in range(1, 8) for b in range(8)]
    rows += [(a, 0) for a in range(8, k)]
    return np.array([a * k + b if (a + 1) * (b + 1) <= PEER_TOPK else -1 for a, b in rows], np.int32)


def _select_rows(tab, row):
    riota = lax.broadcasted_iota(I32, tab.shape, 0)
    return jnp.sum(jnp.where(riota == row, tab, 0), axis=0, keepdims=True)


def _peer_topk_kernel(q_ref, keys_ref, pos_ref, e_ref, g_ref, *, idx_offset):
    kh = PEER_TOPK_HALF
    pos_tab = pos_ref[...]
    big = kh * kh
    e_rows, g_rows = [], []
    for h in range(PEER_HEADS):
        half_v, half_i = [], []
        for p in range(2):
            c0 = (h * 2 + p) * PEER_NKEYS
            qh = q_ref[:, c0:c0 + PEER_NKEYS].astype(BF16)
            sc = _dot_nt(keys_ref[h, p], qh)
            v, ix = _topk_rows(sc, kh)
            half_v.append(v)
            half_i.append(ix)
        v0, v1 = half_v
        cand = jnp.concatenate(
            [v0[0:1] + v1] + [v0[a:a + 1] + v1[0:8] for a in range(1, 8)] + [v0[8:kh] + v1[0:1]], axis=0)
        cand = jnp.where(pos_tab >= 0, cand, -jnp.inf)
        pos_key = jnp.where(pos_tab >= 0, pos_tab, big)
        tops = []
        for _ in range(PEER_TOPK):
            m = jnp.max(cand, axis=0, keepdims=True)
            pos = jnp.min(jnp.where(cand == m, pos_key, big), axis=0, keepdims=True)
            cand = jnp.where(pos_key == pos, -jnp.inf, cand)
            tops.append(m)
            e_rows.append(_select_rows(half_i[0], pos // kh) * PEER_NKEYS
                          + _select_rows(half_i[1], pos % kh) + idx_offset)
        top_s = jnp.concatenate(tops, axis=0)
        ex = jnp.exp(top_s - top_s[0:1, :])
        g_rows.append(ex / jnp.sum(ex, axis=0, keepdims=True))
    e_ref[...] = jnp.concatenate(e_rows, axis=0).T
    g_ref[...] = jnp.concatenate(g_rows, axis=0).T


def _peer_topk(q, keys_bf, idx_offset, t=128):
    n = q.shape[0]
    pos_tab = jnp.asarray(np.tile(_peer_candidates()[:, None], (1, t)))
    out = pl.BlockSpec((t, PEER_SEL), lambda i: (i, 0))
    return pl.pallas_call(
        functools.partial(_peer_topk_kernel, idx_offset=idx_offset), grid=(n // t,),
        in_specs=[pl.BlockSpec((t, PEER_HEADS * PEER_QDIM), lambda i: (i, 0)),
                  pl.BlockSpec(keys_bf.shape, lambda i: (0, 0, 0, 0)),
                  pl.BlockSpec(pos_tab.shape, lambda i: (0, 0))],
        out_specs=[out, out],
        out_shape=[jax.ShapeDtypeStruct((n, PEER_SEL), I32), jax.ShapeDtypeStruct((n, PEER_SEL), F32)],
        compiler_params=_params("parallel"), name="peer_topk",
    )(q, keys_bf, pos_tab)


PEER_SEL = PEER_HEADS * PEER_TOPK

SC_CORES = 2
SC_SUBCORES = 16
SC_LANES = 16
SC_WORKERS = SC_CORES * SC_SUBCORES
SC_ROWS = 32
SC_TOK = 8
SC_CHUNKS = SC_TOK * PEER_SEL // SC_ROWS
SC_VCOLS = 256
SC_UNROLL = 4


def _sc_mesh():
    return plsc.VectorSubcoreMesh(core_axis_name="c", subcore_axis_name="s",
                                  num_cores=SC_CORES, num_subcores=SC_SUBCORES)


def _sc_chunk_pipeline(gather, compute):
    gather(0, 0).start()

    @pl.loop(0, SC_CHUNKS, step=2)
    def _(q):
        gather(q + 1, 1).start()
        gather(q, 0).wait()
        compute(q, 0)

        @pl.when(q + 2 < SC_CHUNKS)
        def _():
            gather(q + 2, 0).start()

        gather(q + 1, 1).wait()
        compute(q + 1, 1)


def _sc_u_body(idx_hbm, h_hbm, u_hbm, act_hbm, idx_v, h_v, rows_v, act_v, sem0, sem1, *, tpw, d):
    tok_base = (lax.axis_index("s") * SC_CORES + lax.axis_index("c")) * tpw
    sems = (sem0, sem1)

    def gather(q, b):
        return pltpu.make_async_copy(u_hbm.at[idx_v.at[pl.ds(q * SC_ROWS, SC_ROWS)]], rows_v.at[b], sems[b])

    def compute(q, b):
        tok = q // (PEER_SEL // SC_ROWS)
        lane = lax.iota(I32, SC_LANES)
        for g in range(SC_ROWS // SC_LANES):
            def body(c, accs, g=g):
                xv = h_v[tok, pl.ds(c * SC_LANES, SC_LANES)]
                return tuple(a + rows_v[b, g * SC_LANES + r, pl.ds(c * SC_LANES, SC_LANES)] * xv
                             for r, a in enumerate(accs))
            accs = lax.fori_loop(0, d // SC_LANES, body,
                                 tuple(jnp.zeros((SC_LANES,), F32) for _ in range(SC_LANES)),
                                 unroll=SC_UNROLL)
            out = jnp.zeros((SC_LANES,), F32)
            for r in range(SC_LANES):
                out = jnp.where(lane == r, jnp.sum(accs[r]), out)
            act_v[pl.ds(q * SC_ROWS + g * SC_LANES, SC_LANES)] = out

    @pl.loop(0, tpw // SC_TOK)
    def _(blk):
        tok0 = tok_base + blk * SC_TOK
        pltpu.sync_copy(idx_hbm.at[pl.ds(tok0 * PEER_SEL, SC_TOK * PEER_SEL)], idx_v)
        pltpu.sync_copy(h_hbm.at[pl.ds(tok0, SC_TOK)], h_v)
        _sc_chunk_pipeline(gather, compute)
        pltpu.sync_copy(act_v, act_hbm.at[pl.ds(tok0 * PEER_SEL, SC_TOK * PEER_SEL)])


def _sc_peer_u(idx_flat, h, u_tab):
    n, d = h.shape
    assert n % (SC_WORKERS * SC_TOK) == 0 and d % SC_VCOLS == 0
    body = functools.partial(_sc_u_body, tpw=n // SC_WORKERS, d=d)
    return pl.kernel(
        body,
        out_type=jax.ShapeDtypeStruct((n * PEER_SEL,), F32),
        mesh=_sc_mesh(),
        scratch_types=[pltpu.VMEM((SC_TOK * PEER_SEL,), I32),
                       pltpu.VMEM((SC_TOK, d), F32),
                       pltpu.VMEM((2, SC_ROWS, d), F32),
                       pltpu.VMEM((SC_TOK * PEER_SEL,), F32),
                       pltpu.SemaphoreType.DMA, pltpu.SemaphoreType.DMA],
        compiler_params=pltpu.CompilerParams(needs_layout_passes=False),
        name="sc_peer_u",
    )(idx_flat, h, u_tab)


def _sc_v_body(idx_hbm, w_hbm, x_hbm, v_hbm, o_hbm, idx_v, w_v, out_v, rows_v, sem0, sem1, *, tpw, d):
    tok_base = (lax.axis_index("s") * SC_CORES + lax.axis_index("c")) * tpw
    sems = (sem0, sem1)
    nacc = SC_VCOLS // SC_LANES

    def gather(q, b):
        return pltpu.make_async_copy(v_hbm.at[idx_v.at[pl.ds(q * SC_ROWS, SC_ROWS)]], rows_v.at[b], sems[b])

    def compute(q, b):
        tok = q // (PEER_SEL // SC_ROWS)
        for cp in range(d // SC_VCOLS):
            col0 = cp * SC_VCOLS

            def body(r, accs, col0=col0):
                wv = plsc.load_gather(w_v, [jnp.full((SC_LANES,), q * SC_ROWS + r, I32)])
                return tuple(acc + wv * rows_v[b, r, pl.ds(col0 + a * SC_LANES, SC_LANES)]
                             for a, acc in enumerate(accs))

            accs = lax.fori_loop(
                0, SC_ROWS, body,
                tuple(out_v[tok, pl.ds(col0 + a * SC_LANES, SC_LANES)] for a in range(nacc)))
            for a in range(nacc):
                out_v[tok, pl.ds(col0 + a * SC_LANES, SC_LANES)] = accs[a]

    @pl.loop(0, tpw // SC_TOK)
    def _(blk):
        tok0 = tok_base + blk * SC_TOK
        pltpu.sync_copy(idx_hbm.at[pl.ds(tok0 * PEER_SEL, SC_TOK * PEER_SEL)], idx_v)
        pltpu.sync_copy(w_hbm.at[pl.ds(tok0 * PEER_SEL, SC_TOK * PEER_SEL)], w_v)
        pltpu.sync_copy(x_hbm.at[pl.ds(tok0, SC_TOK)], out_v)
        _sc_chunk_pipeline(gather, compute)
        pltpu.sync_copy(out_v, o_hbm.at[pl.ds(tok0, SC_TOK)])


def _sc_peer_v(idx_flat, w_flat, x, v_tab):
    n, d = x.shape
    assert n % (SC_WORKERS * SC_TOK) == 0 and d % SC_VCOLS == 0
    body = functools.partial(_sc_v_body, tpw=n // SC_WORKERS, d=d)
    return pl.kernel(
        body,
        out_type=jax.ShapeDtypeStruct((n, d), F32),
        mesh=_sc_mesh(),
        scratch_types=[pltpu.VMEM((SC_TOK * PEER_SEL,), I32),
                       pltpu.VMEM((SC_TOK * PEER_SEL,), F32),
                       pltpu.VMEM((SC_TOK, d), F32),
                       pltpu.VMEM((2, SC_ROWS, d), F32),
                       pltpu.SemaphoreType.DMA, pltpu.SemaphoreType.DMA],
        compiler_params=pltpu.CompilerParams(needs_layout_passes=False),
        name="sc_peer_v",
    )(idx_flat, w_flat, x, v_tab)


def _peer_gate_kernel(g_ref, act_ref, w_ref):
    act = act_ref[...]
    w_ref[...] = g_ref[...] * (0.5 * act * (1.0 + lax.erf(act * (2.0 ** -0.5))))


def _peer_gate(g, act, tm=2048):
    n = g.shape[0]
    spec = pl.BlockSpec((tm, PEER_SEL), lambda i: (i, 0))
    return pl.pallas_call(
        _peer_gate_kernel, grid=(n // tm,), in_specs=[spec, spec], out_specs=spec,
        out_shape=jax.ShapeDtypeStruct((n, PEER_SEL), F32),
        compiler_params=_params("parallel"), name="peer_gate",
    )(g, act)


def _pad_cols(w, width):
    return jnp.pad(w, ((0, 0), (0, width - w.shape[1])))


def _even_weights(w_in):
    hd = A_HEADS * A_HEAD_DIM
    cols = np.cumsum([0, hd, A_HEAD_DIM, A_HEAD_DIM, IDX_HEADS * IDX_DIM, IDX_DIM, IDX_HEADS,
                      B_HEADS * B_HEAD_DIM, B_HEADS * B_HEAD_DIM, B_HEADS * B_HEAD_DIM])
    qa, ka, va, iq, ik, iw, qb, kb, vb = [w_in[:, cols[j]:cols[j + 1]] for j in range(9)]
    vaiw = _pad_cols(jnp.concatenate([va, iw], axis=1), LANES)
    w = jnp.concatenate([qa, iq, qb, kb, vb, ka, ik, vaiw], axis=1).astype(BF16)
    groups = [(0, 512, "r64"), (512, 512, "r64"), (1024, 512, "r64"), (1536, 512, "r64"),
              (2048, 512, None), (2560, LANES, "r64"), (2560 + LANES, LANES, None)]
    return w, groups


def _even_mixer(x, gain, w_in, w_out, tables, batch, s_len):
    w, groups = _even_weights(w_in)
    qa, iq, qb, kb, vb, kaik, vaiw = _fused_proj(x, gain, w, groups, tables)
    out_a = _dsa_attention(iq, qa, kaik, vaiw, batch, s_len)
    out_b = _moba_attention(qb, kb, vb, batch, s_len)
    hd = A_HEADS * A_HEAD_DIM
    w_out = w_out.astype(BF16)
    return _outproj_residual(x, [out_a, out_b], [w_out[:hd], w_out[hd:]])


def _mla_mixer(x, gain, w_in, q_norm, kv_norm, w_uq, w_ukv, w_out, tables, batch, s_len):
    w1 = _pad_cols(w_in, C_Q_RANK + C_KV_RANK + LANES).astype(BF16)
    groups1 = [(0, C_Q_RANK, None), (C_Q_RANK, C_KV_RANK, None), (C_Q_RANK + C_KV_RANK, LANES, "r32")]
    cq, ckv, kr = _fused_proj(x, gain, w1, groups1, tables)
    qd = C_NOPE + C_ROPE
    nope_cols = np.concatenate([np.arange(h * qd, h * qd + C_NOPE) for h in range(C_HEADS)])
    rope_cols = np.concatenate([np.arange(h * qd + C_NOPE, (h + 1) * qd) for h in range(C_HEADS)])
    w2 = jnp.concatenate([w_uq[:, nope_cols], w_uq[:, rope_cols]], axis=1).astype(BF16)
    groups2 = [(0, C_HEADS * C_NOPE, None), (C_HEADS * C_NOPE, C_HEADS * C_ROPE, "r32")]
    qn, qr = _fused_proj(cq, q_norm, w2, groups2, tables)
    kd = C_NOPE + C_V
    kn_cols = np.concatenate([np.arange(h * kd, h * kd + C_NOPE) for h in range(C_HEADS)])
    v_cols = np.concatenate([np.arange(h * kd + C_NOPE, (h + 1) * kd) for h in range(C_HEADS)])
    w3 = jnp.concatenate([w_ukv[:, kn_cols], w_ukv[:, v_cols]], axis=1).astype(BF16)
    groups3 = [(0, C_HEADS * C_NOPE, None), (C_HEADS * C_NOPE, C_HEADS * C_V, None)]
    kn, v = _fused_proj(ckv, kv_norm, w3, groups3, tables)
    o = _mla_attention(qn, qr, kn, kr, v, batch, s_len)
    return _outproj_residual(x, [o], [w_out.astype(BF16)])


def _peer_ffn(x, gain, w_q, sub_keys, u_all, v_all, layer):
    n, d = x.shape
    n_experts = PEER_NKEYS * PEER_NKEYS
    q, h = _fused_proj(x, gain, w_q.astype(BF16), [(0, PEER_HEADS * PEER_QDIM, None)], {}, emit_h=True)
    idx, g = _peer_topk(q, sub_keys.astype(BF16), layer * n_experts)
    idx_flat = idx.reshape(n * PEER_SEL)
    act = _sc_peer_u(idx_flat, h, u_all)
    w = _peer_gate(g, act.reshape(n, PEER_SEL))
    return _sc_peer_v(idx_flat, w.reshape(n * PEER_SEL), x, v_all)


def kernel(x, positions, attn_norm, ffn_norm, final_norm, hyb_w_in, hyb_w_out, mla_w_in, mla_q_norm,
           mla_kv_norm, mla_w_uq, mla_w_ukv, mla_w_out, peer_w_q, peer_sub_keys, peer_u, peer_v):
    batch, s_len, d = x.shape
    depth = attn_norm.shape[0]
    u_all = peer_u.reshape(-1, d)
    v_all = peer_v.reshape(-1, d)
    gb = batch // BATCH_GROUPS
    gn = gb * s_len
    xs, tabs = [], []
    for g in range(BATCH_GROUPS):
        pos = positions[g * gb:(g + 1) * gb].reshape(gn, 1).astype(I32)
        tabs.append({"r64": _rope_tables(pos, A_HEAD_DIM), "r32": _rope_tables(pos, C_ROPE)})
        xs.append(x[g * gb:(g + 1) * gb].reshape(gn, d))
    for i in range(depth):
        j = i // 2
        for g in range(BATCH_GROUPS):
            if i % 2 == 0:
                xs[g] = _even_mixer(xs[g], attn_norm[i], hyb_w_in[j], hyb_w_out[j], tabs[g], gb, s_len)
            else:
                xs[g] = _mla_mixer(xs[g], attn_norm[i], mla_w_in[j], mla_q_norm[j], mla_kv_norm[j],
                                   mla_w_uq[j], mla_w_ukv[j], mla_w_out[j], tabs[g], gb, s_len)
            xs[g] = _peer_ffn(xs[g], ffn_norm[i], peer_w_q[i], peer_sub_keys[i], u_all, v_all, i)
    outs = [_rmsnorm(xg, final_norm).reshape(gb, s_len, d) for xg in xs]
    return jnp.concatenate(outs, axis=0)
```

```python
import functools

import numpy as np
import jax
import jax.numpy as jnp
from jax import lax
from jax.experimental import pallas as pl
from jax.experimental.pallas import tpu as pltpu
from jax.experimental.pallas import tpu_sc as plsc

NORM_EPS = 1e-6
ROPE_THETA = 10000.0

A_HEADS = 8
A_HEAD_DIM = 64
IDX_HEADS = 8
IDX_DIM = 64
DSA_TOPK_MAX = 256

B_HEADS = 8
B_HEAD_DIM = 64
MOBA_BLOCK = 256
MOBA_TOPK = 3

C_HEADS = 16
C_NOPE = 64
C_ROPE = 32
C_V = 64
C_Q_RANK = 768
C_KV_RANK = 256

PEER_HEADS = 8
PEER_NKEYS = 128
PEER_QDIM = 256
PEER_TOPK_HALF = 16
PEER_TOPK = 16

BATCH_GROUPS = 2
LANES = 128
VMEM_LIMIT = 56 * 1024 * 1024

F32 = jnp.float32
BF16 = jnp.bfloat16
I32 = jnp.int32
INT_MIN = -(2 ** 31)


def _params(*sem):
    return pltpu.CompilerParams(dimension_semantics=sem, vmem_limit_bytes=VMEM_LIMIT)


def _dot(a, b):
    return jnp.dot(a, b, preferred_element_type=F32)


def _dot_nt(a, b):
    return lax.dot_general(a, b, (((1,), (1,)), ((), ())), preferred_element_type=F32)


def _rope_table_kernel(pos_ref, inv_ref, sign_ref, cos_ref, sin_ref):
    ang = pos_ref[...].astype(F32) * inv_ref[...]
    cos_ref[...] = jnp.cos(ang)
    sin_ref[...] = jnp.sin(ang) * sign_ref[...]


def _rope_tables(pos, dim, tm=1024):
    n = pos.shape[0]
    half = dim // 2
    inv = 1.0 / (ROPE_THETA ** (jnp.arange(0, dim, 2, dtype=F32) / dim))
    reps = LANES // dim
    inv_l = jnp.tile(jnp.concatenate([inv, inv]), reps)[None, :]
    sign_l = jnp.tile(jnp.concatenate([-jnp.ones(half, F32), jnp.ones(half, F32)]), reps)[None, :]
    out = jax.ShapeDtypeStruct((n, LANES), F32)
    return pl.pallas_call(
        _rope_table_kernel,
        grid=(n // tm,),
        in_specs=[pl.BlockSpec((tm, 1), lambda i: (i, 0)),
                  pl.BlockSpec((1, LANES), lambda i: (0, 0)),
                  pl.BlockSpec((1, LANES), lambda i: (0, 0))],
        out_specs=[pl.BlockSpec((tm, LANES), lambda i: (i, 0))] * 2,
        out_shape=[out, out],
        compiler_params=_params("parallel"),
        name="rope_tables",
    )(pos, inv_l, sign_l)


def _rope_epilogue(acc, cos, sin, half):
    wd = acc.shape[1]
    reps = wd // LANES
    if reps > 1:
        cos = jnp.concatenate([cos] * reps, axis=1)
        sin = jnp.concatenate([sin] * reps, axis=1)
    lane = lax.broadcasted_iota(I32, acc.shape, 1)
    first = (lane % (2 * half)) < half
    partner = jnp.where(first, pltpu.roll(acc, wd - half, 1), pltpu.roll(acc, half, 1))
    return acc * cos + partner * sin


def _proj_kernel(*refs, groups, has_gain, emit_h, tab_names):
    it = iter(refs)
    x_ref = next(it)
    g_ref = next(it) if has_gain else None
    w_ref = next(it)
    tabs = {name: (next(it), next(it)) for name in tab_names}
    out_refs = [next(it) for _ in groups]
    h_ref = next(it) if emit_h else None

    x = x_ref[...]
    if has_gain:
        ms = jnp.mean(x * x, axis=-1, keepdims=True)
        h = x * lax.rsqrt(ms + NORM_EPS) * g_ref[...]
    else:
        h = x
    if emit_h:
        h_ref[...] = h
    hb = h.astype(BF16)
    for o_ref, (c0, wd, rope) in zip(out_refs, groups):
        acc = _dot(hb, w_ref[:, c0:c0 + wd])
        if rope is not None:
            cos_ref, sin_ref = tabs[rope]
            acc = _rope_epilogue(acc, cos_ref[...], sin_ref[...], 32 if rope == "r64" else 16)
        o_ref[...] = acc


def _fused_proj(x, gain, w, groups, tables, emit_h=False, tm=256):
    n, k = x.shape
    tab_names = sorted({g[2] for g in groups if g[2] is not None})
    args = [x]
    in_specs = [pl.BlockSpec((tm, k), lambda i: (i, 0))]
    if gain is not None:
        args.append(gain.reshape(1, k).astype(F32))
        in_specs.append(pl.BlockSpec((1, k), lambda i: (0, 0)))
    args.append(w)
    in_specs.append(pl.BlockSpec(w.shape, lambda i: (0, 0)))
    for name in tab_names:
        for t in tables[name]:
            args.append(t)
            in_specs.append(pl.BlockSpec((tm, LANES), lambda i: (i, 0)))
    out_shape = [jax.ShapeDtypeStruct((n, wd), F32) for (_, wd, _) in groups]
    out_specs = [pl.BlockSpec((tm, wd), lambda i: (i, 0)) for (_, wd, _) in groups]
    if emit_h:
        out_shape.append(jax.ShapeDtypeStruct((n, k), F32))
        out_specs.append(pl.BlockSpec((tm, k), lambda i: (i, 0)))
    kern = functools.partial(_proj_kernel, groups=tuple(groups), has_gain=gain is not None,
                             emit_h=emit_h, tab_names=tuple(tab_names))
    return pl.pallas_call(
        kern, grid=(n // tm,), in_specs=in_specs, out_specs=out_specs, out_shape=out_shape,
        compiler_params=_params("parallel"), name="fused_proj",
    )(*args)


def _outproj_kernel(*refs, n_in):
    res_ref = refs[0]
    a_refs = refs[1:1 + n_in]
    w_refs = refs[1 + n_in:1 + 2 * n_in]
    o_ref = refs[-1]
    acc = res_ref[...]
    for a_ref, w_ref in zip(a_refs, w_refs):
        acc = acc + _dot(a_ref[...].astype(BF16), w_ref[...])
    o_ref[...] = acc


def _outproj_residual(res, acts, ws, tm=256):
    n, d = res.shape
    in_specs = [pl.BlockSpec((tm, d), lambda i: (i, 0))]
    in_specs += [pl.BlockSpec((tm, a.shape[1]), lambda i: (i, 0)) for a in acts]
    in_specs += [pl.BlockSpec(w.shape, lambda i: (0, 0)) for w in ws]
    return pl.pallas_call(
        functools.partial(_outproj_kernel, n_in=len(acts)),
        grid=(n // tm,), in_specs=in_specs,
        out_specs=pl.BlockSpec((tm, d), lambda i: (i, 0)),
        out_shape=jax.ShapeDtypeStruct((n, d), F32),
        compiler_params=_params("parallel"), name="outproj_residual",
    )(res, *acts, *ws)


def _rmsnorm_kernel(x_ref, g_ref, o_ref):
    x = x_ref[...]
    ms = jnp.mean(x * x, axis=-1, keepdims=True)
    o_ref[...] = x * lax.rsqrt(ms + NORM_EPS) * g_ref[...]


def _rmsnorm(x, gain, tm=512):
    n, d = x.shape
    return pl.pallas_call(
        _rmsnorm_kernel, grid=(n // tm,),
        in_specs=[pl.BlockSpec((tm, d), lambda i: (i, 0)), pl.BlockSpec((1, d), lambda i: (0, 0))],
        out_specs=pl.BlockSpec((tm, d), lambda i: (i, 0)),
        out_shape=jax.ShapeDtypeStruct((n, d), F32),
        compiler_params=_params("parallel"), name="final_rmsnorm",
    )(x, gain.reshape(1, d))


def _softmax_pv(s, mask, v_bf):
    s = jnp.where(mask, s, -jnp.inf)
    m = jnp.max(s, axis=-1, keepdims=True)
    p = jnp.exp(s - m)
    l = jnp.sum(p, axis=-1, keepdims=True)
    return _dot(p.astype(BF16), v_bf) / l


def _dsa_kernel(iq_ref, qa_ref, kaik_ref, va_ref, iw_ref, o_ref, *, n_sel, tq, s_len):
    i = pl.program_id(1)
    kaik = kaik_ref[...]
    ka = kaik[:, 0:A_HEAD_DIM].astype(BF16)
    ik = kaik[:, A_HEAD_DIM:A_HEAD_DIM + IDX_DIM].astype(BF16)
    va = va_ref[:, 0:A_HEAD_DIM].astype(BF16)
    iw = iw_ref[:, A_HEAD_DIM:A_HEAD_DIM + IDX_HEADS]
    iq = iq_ref[...]

    score = jnp.zeros((tq, s_len), F32)
    for h in range(IDX_HEADS):
        lg = _dot_nt(iq[:, h * IDX_DIM:(h + 1) * IDX_DIM].astype(BF16), ik)
        score = score + jnp.maximum(lg, 0.0) * iw[:, h:h + 1]
    w_scale = (IDX_HEADS ** -0.5) * (IDX_DIM ** -0.5)
    score = score * w_scale + 0.0

    key_pos = lax.broadcasted_iota(I32, (tq, s_len), 1)
    q_pos = i * tq + lax.broadcasted_iota(I32, (tq, s_len), 0)
    causal = key_pos <= q_pos

    bits = pltpu.bitcast(score, I32)
    key = jnp.where(bits < 0, bits ^ 0x7FFFFFFF, bits)
    key = jnp.where(causal, key, INT_MIN)

    def count_ge(cand):
        return jnp.sum(jnp.where(key >= cand, 1.0, 0.0), axis=-1, keepdims=True)

    lo = jnp.where(count_ge(jnp.zeros((tq, 1), I32)) >= n_sel, 0, INT_MIN).astype(I32)

    def bisect(t, lo):
        cand = lo + jnp.left_shift(jnp.int32(1), 30 - t)
        return jnp.where(count_ge(cand) >= n_sel, cand, lo)

    thr = lax.fori_loop(0, 31, bisect, lo)

    gt = key > thr
    eq = key == thr
    need = n_sel - jnp.sum(jnp.where(gt, 1.0, 0.0), axis=-1, keepdims=True)
    r = lax.broadcasted_iota(I32, (LANES, LANES), 0)
    c = lax.broadcasted_iota(I32, (LANES, LANES), 1)
    tri = jnp.where(r < c, 1.0, 0.0).astype(BF16)
    eq_bf = jnp.where(eq, 1.0, 0.0).astype(BF16)
    carry = jnp.zeros((tq, 1), F32)
    pref = []
    for cb in range(s_len // LANES):
        blk = eq_bf[:, cb * LANES:(cb + 1) * LANES]
        pref.append(_dot(blk, tri) + carry)
        carry = carry + jnp.sum(blk.astype(F32), axis=-1, keepdims=True)
    prefix = jnp.concatenate(pref, axis=1)
    sel = (gt | (eq & (prefix < need))) & causal

    a_scale = A_HEAD_DIM ** -0.5
    qa = qa_ref[...]
    outs = []
    for h in range(A_HEADS):
        sc = _dot_nt(qa[:, h * A_HEAD_DIM:(h + 1) * A_HEAD_DIM].astype(BF16), ka) * a_scale
        outs.append(_softmax_pv(sc, sel, va))
    o_ref[...] = jnp.concatenate(outs, axis=1)


def _dsa_attention(iq, qa, kaik, vaiw, batch, s_len, tq=128):
    n = iq.shape[0]
    nq = s_len // tq
    n_sel = min(DSA_TOPK_MAX, s_len // 4)
    hd = A_HEADS * A_HEAD_DIM
    kern = functools.partial(_dsa_kernel, n_sel=n_sel, tq=tq, s_len=s_len)
    return pl.pallas_call(
        kern, grid=(batch, nq),
        in_specs=[pl.BlockSpec((tq, IDX_HEADS * IDX_DIM), lambda b, i: (b * nq + i, 0)),
                  pl.BlockSpec((tq, hd), lambda b, i: (b * nq + i, 0)),
                  pl.BlockSpec((s_len, LANES), lambda b, i: (b, 0)),
                  pl.BlockSpec((s_len, LANES), lambda b, i: (b, 0)),
                  pl.BlockSpec((tq, LANES), lambda b, i: (b * nq + i, 0))],
        out_specs=pl.BlockSpec((tq, hd), lambda b, i: (b * nq + i, 0)),
        out_shape=jax.ShapeDtypeStruct((n, hd), F32),
        compiler_params=_params("parallel", "arbitrary"), name="dsa_attention",
    )(iq, qa, kaik, vaiw, vaiw)


def _moba_kernel(q_ref, k_ref, v_ref, o_ref, *, n_top, s_len):
    i = pl.program_id(2)
    tq = MOBA_BLOCK
    nb = s_len // MOBA_BLOCK
    d = B_HEAD_DIM
    scale = d ** -0.5
    key_pos = lax.broadcasted_iota(I32, (tq, s_len), 1)
    q_pos = i * tq + lax.broadcasted_iota(I32, (tq, s_len), 0)
    own_mask = (key_pos >= i * MOBA_BLOCK) & (key_pos <= q_pos)
    blk_lane = lax.broadcasted_iota(I32, (tq, nb), 1)
    expand = jnp.where(lax.broadcasted_iota(I32, (nb, s_len), 1) // MOBA_BLOCK
                       == lax.broadcasted_iota(I32, (nb, s_len), 0), 1.0, 0.0).astype(BF16)
    outs = []
    for hh in range(LANES // d):
        q = q_ref[:, hh * d:(hh + 1) * d].astype(BF16)
        k = k_ref[:, hh * d:(hh + 1) * d]
        v = v_ref[:, hh * d:(hh + 1) * d].astype(BF16)
        k_mean = jnp.sum(k.reshape(nb, MOBA_BLOCK, d), axis=1) / MOBA_BLOCK
        gate = _dot_nt(q, k_mean.astype(BF16))
        gate = jnp.where(blk_lane < i, gate, -jnp.inf)
        rank = jnp.zeros((tq, nb), F32)
        for b2 in range(nb):
            gb = gate[:, b2:b2 + 1]
            beats = (gb > gate) | ((gb == gate) & (b2 < blk_lane))
            rank = rank + jnp.where(beats, 1.0, 0.0)
        sel = jnp.where((rank < n_top) & (blk_lane < i), 1.0, 0.0).astype(BF16)
        selk = _dot(sel, expand)
        mask = (selk > 0.5) | own_mask
        s = _dot_nt(q, k.astype(BF16)) * scale
        outs.append(_softmax_pv(s, mask, v))
    o_ref[...] = jnp.concatenate(outs, axis=1)


def _moba_attention(qb, kb, vb, batch, s_len):
    n = qb.shape[0]
    nb = s_len // MOBA_BLOCK
    n_top = min(MOBA_TOPK, nb - 1)
    hp = B_HEADS * B_HEAD_DIM // LANES
    kern = functools.partial(_moba_kernel, n_top=n_top, s_len=s_len)
    return pl.pallas_call(
        kern, grid=(batch, hp, nb),
        in_specs=[pl.BlockSpec((MOBA_BLOCK, LANES), lambda b, g, i: (b * nb + i, g)),
                  pl.BlockSpec((s_len, LANES), lambda b, g, i: (b, g)),
                  pl.BlockSpec((s_len, LANES), lambda b, g, i: (b, g))],
        out_specs=pl.BlockSpec((MOBA_BLOCK, LANES), lambda b, g, i: (b * nb + i, g)),
        out_shape=jax.ShapeDtypeStruct((n, B_HEADS * B_HEAD_DIM), F32),
        compiler_params=_params("parallel", "parallel", "arbitrary"), name="moba_attention",
    )(qb, kb, vb)


MLA_GROUP = 4


def _mla_kernel(qn_ref, qr_ref, kn_ref, kr_ref, v_ref, o_ref, *, tq, s_len):
    i = pl.program_id(2)
    scale = (C_NOPE + C_ROPE) ** -0.5
    key_pos = lax.broadcasted_iota(I32, (tq, s_len), 1)
    q_pos = i * tq + lax.broadcasted_iota(I32, (tq, s_len), 0)
    causal = key_pos <= q_pos
    kr = kr_ref[:, 0:C_ROPE].astype(BF16)
    outs = []
    for hh in range(MLA_GROUP):
        qn = qn_ref[:, hh * C_NOPE:(hh + 1) * C_NOPE].astype(BF16)
        qr = qr_ref[:, hh * C_ROPE:(hh + 1) * C_ROPE].astype(BF16)
        kn = kn_ref[:, hh * C_NOPE:(hh + 1) * C_NOPE].astype(BF16)
        v = v_ref[:, hh * C_V:(hh + 1) * C_V].astype(BF16)
        s = (_dot_nt(qn, kn) + _dot_nt(qr, kr)) * scale
        outs.append(_softmax_pv(s, causal, v))
    o_ref[...] = jnp.concatenate(outs, axis=1)


def _mla_attention(qn, qr, kn, kr, v, batch, s_len, tq=256):
    n = qn.shape[0]
    nq = s_len // tq
    ng = C_HEADS // MLA_GROUP
    kern = functools.partial(_mla_kernel, tq=tq, s_len=s_len)
    return pl.pallas_call(
        kern, grid=(batch, ng, nq),
        in_specs=[pl.BlockSpec((tq, MLA_GROUP * C_NOPE), lambda b, g, i: (b * nq + i, g)),
                  pl.BlockSpec((tq, MLA_GROUP * C_ROPE), lambda b, g, i: (b * nq + i, g)),
                  pl.BlockSpec((s_len, MLA_GROUP * C_NOPE), lambda b, g, i: (b, g)),
                  pl.BlockSpec((s_len, LANES), lambda b, g, i: (b, 0)),
                  pl.BlockSpec((s_len, MLA_GROUP * C_V), lambda b, g, i: (b, g))],
        out_specs=pl.BlockSpec((tq, MLA_GROUP * C_V), lambda b, g, i: (b * nq + i, g)),
        out_shape=jax.ShapeDtypeStruct((n, C_HEADS * C_V), F32),
        compiler_params=_params("parallel", "parallel", "arbitrary"), name="mla_attention",
    )(qn, qr, kn, kr, v)


def _topk_rows(sc, k):
    r = sc.shape[0]
    riota = lax.broadcasted_iota(I32, sc.shape, 0)
    vals, rows = [], []
    for _ in range(k):
        m = jnp.max(sc, axis=0, keepdims=True)
        pos = jnp.min(jnp.where(sc == m, riota, r), axis=0, keepdims=True)
        sc = jnp.where(riota == pos, -jnp.inf, sc)
        vals.append(m)
        rows.append(pos)
    return jnp.concatenate(vals, axis=0), jnp.concatenate(rows, axis=0)


def _peer_candidates():
    k = PEER_TOPK_HALF
    rows = [(0, b) for b in range(k)]
    rows += [(a, b) for a in range(1, 8) for b in range(8)]
    rows += [(a, 0) for a in range(8, k)]
    return np.array([a * k + b if (a + 1) * (b + 1) <= PEER_TOPK else -1 for a, b in rows], np.int32)


def _select_rows(tab, row):
    riota = lax.broadcasted_iota(I32, tab.shape, 0)
    return jnp.sum(jnp.where(riota == row, tab, 0), axis=0, keepdims=True)


def _peer_topk_kernel(q_ref, keys_ref, pos_ref, e_ref, g_ref, *, idx_offset):
    kh = PEER_TOPK_HALF
    pos_tab = pos_ref[...]
    big = kh * kh
    e_rows, g_rows = [], []
    for h in range(PEER_HEADS):
        half_v, half_i = [], []
        for p in range(2):
            c0 = (h * 2 + p) * PEER_NKEYS
            qh = q_ref[:, c0:c0 + PEER_NKEYS].astype(BF16)
            sc = _dot_nt(keys_ref[h, p], qh)
            v, ix = _topk_rows(sc, kh)
            half_v.append(v)
            half_i.append(ix)
        v0, v1 = half_v
        cand = jnp.concatenate(
            [v0[0:1] + v1] + [v0[a:a + 1] + v1[0:8] for a in range(1, 8)] + [v0[8:kh] + v1[0:1]], axis=0)
        cand = jnp.where(pos_tab >= 0, cand, -jnp.inf)
        pos_key = jnp.where(pos_tab >= 0, pos_tab, big)
        tops = []
        for _ in range(PEER_TOPK):
            m = jnp.max(cand, axis=0, keepdims=True)
            pos = jnp.min(jnp.where(cand == m, pos_key, big), axis=0, keepdims=True)
            cand = jnp.where(pos_key == pos, -jnp.inf, cand)
            tops.append(m)
            e_rows.append(_select_rows(half_i[0], pos // kh) * PEER_NKEYS
                          + _select_rows(half_i[1], pos % kh) + idx_offset)
        top_s = jnp.concatenate(tops, axis=0)
        ex = jnp.exp(top_s - top_s[0:1, :])
        g_rows.append(ex / jnp.sum(ex, axis=0, keepdims=True))
    e_ref[...] = jnp.concatenate(e_rows, axis=0).T
    g_ref[...] = jnp.concatenate(g_rows, axis=0).T


def _peer_topk(q, keys_bf, idx_offset, t=128):
    n = q.shape[0]
    pos_tab = jnp.asarray(np.tile(_peer_candidates()[:, None], (1, t)))
    out = pl.BlockSpec((t, PEER_SEL), lambda i: (i, 0))
    return pl.pallas_call(
        functools.partial(_peer_topk_kernel, idx_offset=idx_offset), grid=(n // t,),
        in_specs=[pl.BlockSpec((t, PEER_HEADS * PEER_QDIM), lambda i: (i, 0)),
                  pl.BlockSpec(keys_bf.shape, lambda i: (0, 0, 0, 0)),
                  pl.BlockSpec(pos_tab.shape, lambda i: (0, 0))],
        out_specs=[out, out],
        out_shape=[jax.ShapeDtypeStruct((n, PEER_SEL), I32), jax.ShapeDtypeStruct((n, PEER_SEL), F32)],
        compiler_params=_params("parallel"), name="peer_topk",
    )(q, keys_bf, pos_tab)


PEER_SEL = PEER_HEADS * PEER_TOPK

SC_CORES = 2
SC_SUBCORES = 16
SC_LANES = 16
SC_WORKERS = SC_CORES * SC_SUBCORES
SC_ROWS = 32
SC_TOK = 8
SC_CHUNKS = SC_TOK * PEER_SEL // SC_ROWS
SC_VCOLS = 256
SC_UNROLL = 1


def _sc_mesh():
    return plsc.VectorSubcoreMesh(core_axis_name="c", subcore_axis_name="s",
                                  num_cores=SC_CORES, num_subcores=SC_SUBCORES)


def _pack_table(t):
    half = t.shape[1] // 2
    b = lax.bitcast_convert_type(t.astype(BF16), jnp.uint16).astype(jnp.uint32)
    return lax.bitcast_convert_type(b[:, :half] | (b[:, half:] << 16), I32)


def _unpack_pair(words):
    lo = lax.bitcast_convert_type(jnp.left_shift(words, 16), F32)
    hi = lax.bitcast_convert_type(jnp.bitwise_and(words, jnp.int32(-65536)), F32)
    return lo, hi


def _sc_chunk_pipeline(gather, compute):
    gather(0, 0).start()

    @pl.loop(0, SC_CHUNKS, step=2)
    def _(q):
        gather(q + 1, 1).start()
        gather(q, 0).wait()
        compute(q, 0)

        @pl.when(q + 2 < SC_CHUNKS)
        def _():
            gather(q + 2, 0).start()

        gather(q + 1, 1).wait()
        compute(q + 1, 1)


def _sc_u_body(idx_hbm, h_hbm, u_hbm, act_hbm, idx_v, h_v, rows_v, act_v, sem0, sem1, *, tpw, d):
    tok_base = (lax.axis_index("s") * SC_CORES + lax.axis_index("c")) * tpw
    sems = (sem0, sem1)

    def gather(q, b):
        return pltpu.make_async_copy(u_hbm.at[idx_v.at[pl.ds(q * SC_ROWS, SC_ROWS)]], rows_v.at[b], sems[b])

    def compute(q, b):
        tok = q // (PEER_SEL // SC_ROWS)
        lane = lax.iota(I32, SC_LANES)
        for g in range(SC_ROWS // SC_LANES):
            def body(c, accs, g=g):
                x_lo = h_v[tok, pl.ds(c * SC_LANES, SC_LANES)]
                x_hi = h_v[tok, pl.ds(d // 2 + c * SC_LANES, SC_LANES)]
                out = []
                for r, a in enumerate(accs):
                    lo, hi = _unpack_pair(rows_v[b, g * SC_LANES + r, pl.ds(c * SC_LANES, SC_LANES)])
                    out.append(a + lo * x_lo + hi * x_hi)
                return tuple(out)
            accs = lax.fori_loop(0, d // (2 * SC_LANES), body,
                                 tuple(jnp.zeros((SC_LANES,), F32) for _ in range(SC_LANES)),
                                 unroll=SC_UNROLL)
            out = jnp.zeros((SC_LANES,), F32)
            for r in range(SC_LANES):
                out = jnp.where(lane == r, jnp.sum(accs[r]), out)
            act_v[pl.ds(q * SC_ROWS + g * SC_LANES, SC_LANES)] = out

    @pl.loop(0, tpw // SC_TOK)
    def _(blk):
        tok0 = tok_base + blk * SC_TOK
        pltpu.sync_copy(idx_hbm.at[pl.ds(tok0 * PEER_SEL, SC_TOK * PEER_SEL)], idx_v)
        pltpu.sync_copy(h_hbm.at[pl.ds(tok0, SC_TOK)], h_v)
        _sc_chunk_pipeline(gather, compute)
        pltpu.sync_copy(act_v, act_hbm.at[pl.ds(tok0 * PEER_SEL, SC_TOK * PEER_SEL)])


def _sc_peer_u(idx_flat, h, u_tab):
    n, d = h.shape
    assert u_tab.shape[1] * 2 == d and u_tab.dtype == I32
    assert n % (SC_WORKERS * SC_TOK) == 0 and d % SC_VCOLS == 0
    body = functools.partial(_sc_u_body, tpw=n // SC_WORKERS, d=d)
    return pl.kernel(
        body,
        out_type=jax.ShapeDtypeStruct((n * PEER_SEL,), F32),
        mesh=_sc_mesh(),
        scratch_types=[pltpu.VMEM((SC_TOK * PEER_SEL,), I32),
                       pltpu.VMEM((SC_TOK, d), F32),
                       pltpu.VMEM((2, SC_ROWS, d // 2), I32),
                       pltpu.VMEM((SC_TOK * PEER_SEL,), F32),
                       pltpu.SemaphoreType.DMA, pltpu.SemaphoreType.DMA],
        compiler_params=pltpu.CompilerParams(needs_layout_passes=False),
        name="sc_peer_u",
    )(idx_flat, h, u_tab)


def _sc_v_body(idx_hbm, w_hbm, x_hbm, v_hbm, o_hbm, idx_v, w_v, out_v, rows_v, sem0, sem1, *, tpw, d):
    tok_base = (lax.axis_index("s") * SC_CORES + lax.axis_index("c")) * tpw
    sems = (sem0, sem1)
    half = d // 2
    nword = SC_VCOLS // (2 * SC_LANES)

    def gather(q, b):
        return pltpu.make_async_copy(v_hbm.at[idx_v.at[pl.ds(q * SC_ROWS, SC_ROWS)]], rows_v.at[b], sems[b])

    def compute(q, b):
        tok = q // (PEER_SEL // SC_ROWS)
        for cp in range(d // SC_VCOLS):
            w0 = cp * nword * SC_LANES
            cols = [half * part + w0 + a * SC_LANES for a in range(nword) for part in range(2)]

            def body(r, accs, w0=w0):
                wv = plsc.load_gather(w_v, [jnp.full((SC_LANES,), q * SC_ROWS + r, I32)])
                out = []
                for a in range(nword):
                    lo, hi = _unpack_pair(rows_v[b, r, pl.ds(w0 + a * SC_LANES, SC_LANES)])
                    out += [accs[2 * a] + wv * lo, accs[2 * a + 1] + wv * hi]
                return tuple(out)

            accs = lax.fori_loop(0, SC_ROWS, body,
                                 tuple(out_v[tok, pl.ds(c, SC_LANES)] for c in cols))
            for c, acc in zip(cols, accs):
                out_v[tok, pl.ds(c, SC_LANES)] = acc

    @pl.loop(0, tpw // SC_TOK)
    def _(blk):
        tok0 = tok_base + blk * SC_TOK
        pltpu.sync_copy(idx_hbm.at[pl.ds(tok0 * PEER_SEL, SC_TOK * PEER_SEL)], idx_v)
        pltpu.sync_copy(w_hbm.at[pl.ds(tok0 * PEER_SEL, SC_TOK * PEER_SEL)], w_v)
        pltpu.sync_copy(x_hbm.at[pl.ds(tok0, SC_TOK)], out_v)
        _sc_chunk_pipeline(gather, compute)
        pltpu.sync_copy(out_v, o_hbm.at[pl.ds(tok0, SC_TOK)])


def _sc_peer_v(idx_flat, w_flat, x, v_tab):
    n, d = x.shape
    assert v_tab.shape[1] * 2 == d and v_tab.dtype == I32
    assert n % (SC_WORKERS * SC_TOK) == 0 and d % SC_VCOLS == 0
    body = functools.partial(_sc_v_body, tpw=n // SC_WORKERS, d=d)
    return pl.kernel(
        body,
        out_type=jax.ShapeDtypeStruct((n, d), F32),
        mesh=_sc_mesh(),
        scratch_types=[pltpu.VMEM((SC_TOK * PEER_SEL,), I32),
                       pltpu.VMEM((SC_TOK * PEER_SEL,), F32),
                       pltpu.VMEM((SC_TOK, d), F32),
                       pltpu.VMEM((2, SC_ROWS, d // 2), I32),
                       pltpu.SemaphoreType.DMA, pltpu.SemaphoreType.DMA],
        compiler_params=pltpu.CompilerParams(needs_layout_passes=False),
        name="sc_peer_v",
    )(idx_flat, w_flat, x, v_tab)


def _peer_gate_kernel(g_ref, act_ref, w_ref):
    act = act_ref[...]
    w_ref[...] = g_ref[...] * (0.5 * act * (1.0 + lax.erf(act * (2.0 ** -0.5))))


def _peer_gate(g, act, tm=2048):
    n = g.shape[0]
    spec = pl.BlockSpec((tm, PEER_SEL), lambda i: (i, 0))
    return pl.pallas_call(
        _peer_gate_kernel, grid=(n // tm,), in_specs=[spec, spec], out_specs=spec,
        out_shape=jax.ShapeDtypeStruct((n, PEER_SEL), F32),
        compiler_params=_params("parallel"), name="peer_gate",
    )(g, act)


def _pad_cols(w, width):
    return jnp.pad(w, ((0, 0), (0, width - w.shape[1])))


def _even_weights(w_in):
    hd = A_HEADS * A_HEAD_DIM
    cols = np.cumsum([0, hd, A_HEAD_DIM, A_HEAD_DIM, IDX_HEADS * IDX_DIM, IDX_DIM, IDX_HEADS,
                      B_HEADS * B_HEAD_DIM, B_HEADS * B_HEAD_DIM, B_HEADS * B_HEAD_DIM])
    qa, ka, va, iq, ik, iw, qb, kb, vb = [w_in[:, cols[j]:cols[j + 1]] for j in range(9)]
    vaiw = _pad_cols(jnp.concatenate([va, iw], axis=1), LANES)
    w = jnp.concatenate([qa, iq, qb, kb, vb, ka, ik, vaiw], axis=1).astype(BF16)
    groups = [(0, 512, "r64"), (512, 512, "r64"), (1024, 512, "r64"), (1536, 512, "r64"),
              (2048, 512, None), (2560, LANES, "r64"), (2560 + LANES, LANES, None)]
    return w, groups


def _even_mixer(x, gain, w_in, w_out, tables, batch, s_len):
    w, groups = _even_weights(w_in)
    qa, iq, qb, kb, vb, kaik, vaiw = _fused_proj(x, gain, w, groups, tables)
    out_a = _dsa_attention(iq, qa, kaik, vaiw, batch, s_len)
    out_b = _moba_attention(qb, kb, vb, batch, s_len)
    hd = A_HEADS * A_HEAD_DIM
    w_out = w_out.astype(BF16)
    return _outproj_residual(x, [out_a, out_b], [w_out[:hd], w_out[hd:]])


def _mla_mixer(x, gain, w_in, q_norm, kv_norm, w_uq, w_ukv, w_out, tables, batch, s_len):
    w1 = _pad_cols(w_in, C_Q_RANK + C_KV_RANK + LANES).astype(BF16)
    groups1 = [(0, C_Q_RANK, None), (C_Q_RANK, C_KV_RANK, None), (C_Q_RANK + C_KV_RANK, LANES, "r32")]
    cq, ckv, kr = _fused_proj(x, gain, w1, groups1, tables)
    qd = C_NOPE + C_ROPE
    nope_cols = np.concatenate([np.arange(h * qd, h * qd + C_NOPE) for h in range(C_HEADS)])
    rope_cols = np.concatenate([np.arange(h * qd + C_NOPE, (h + 1) * qd) for h in range(C_HEADS)])
    w2 = jnp.concatenate([w_uq[:, nope_cols], w_uq[:, rope_cols]], axis=1).astype(BF16)
    groups2 = [(0, C_HEADS * C_NOPE, None), (C_HEADS * C_NOPE, C_HEADS * C_ROPE, "r32")]
    qn, qr = _fused_proj(cq, q_norm, w2, groups2, tables)
    kd = C_NOPE + C_V
    kn_cols = np.concatenate([np.arange(h * kd, h * kd + C_NOPE) for h in range(C_HEADS)])
    v_cols = np.concatenate([np.arange(h * kd + C_NOPE, (h + 1) * kd) for h in range(C_HEADS)])
    w3 = jnp.concatenate([w_ukv[:, kn_cols], w_ukv[:, v_cols]], axis=1).astype(BF16)
    groups3 = [(0, C_HEADS * C_NOPE, None), (C_HEADS * C_NOPE, C_HEADS * C_V, None)]
    kn, v = _fused_proj(ckv, kv_norm, w3, groups3, tables)
    o = _mla_attention(qn, qr, kn, kr, v, batch, s_len)
    return _outproj_residual(x, [o], [w_out.astype(BF16)])


def _peer_ffn(x, gain, w_q, sub_keys, u_all, v_all, layer):
    n, d = x.shape
    n_experts = PEER_NKEYS * PEER_NKEYS
    q, h = _fused_proj(x, gain, w_q.astype(BF16), [(0, PEER_HEADS * PEER_QDIM, None)], {}, emit_h=True)
    idx, g = _peer_topk(q, sub_keys.astype(BF16), layer * n_experts)
    idx_flat = idx.reshape(n * PEER_SEL)
    act = _sc_peer_u(idx_flat, h, u_all)
    w = _peer_gate(g, act.reshape(n, PEER_SEL))
    return _sc_peer_v(idx_flat, w.reshape(n * PEER_SEL), x, v_all)


def kernel(x, positions, attn_norm, ffn_norm, final_norm, hyb_w_in, hyb_w_out, mla_w_in, mla_q_norm,
           mla_kv_norm, mla_w_uq, mla_w_ukv, mla_w_out, peer_w_q, peer_sub_keys, peer_u, peer_v):
    batch, s_len, d = x.shape
    depth = attn_norm.shape[0]
    u_all = _pack_table(peer_u.reshape(-1, d))
    v_all = _pack_table(peer_v.reshape(-1, d))
    gb = batch // BATCH_GROUPS
    gn = gb * s_len
    xs, tabs = [], []
    for g in range(BATCH_GROUPS):
        pos = positions[g * gb:(g + 1) * gb].reshape(gn, 1).astype(I32)
        tabs.append({"r64": _rope_tables(pos, A_HEAD_DIM), "r32": _rope_tables(pos, C_ROPE)})
        xs.append(x[g * gb:(g + 1) * gb].reshape(gn, d))
    for i in range(depth):
        j = i // 2
        for g in range(BATCH_GROUPS):
            if i % 2 == 0:
                xs[g] = _even_mixer(xs[g], attn_norm[i], hyb_w_in[j], hyb_w_out[j], tabs[g], gb, s_len)
            else:
                xs[g] = _mla_mixer(xs[g], attn_norm[i], mla_w_in[j], mla_q_norm[j], mla_kv_norm[j],
                                   mla_w_uq[j], mla_w_ukv[j], mla_w_out[j], tabs[g], gb, s_len)
            xs[g] = _peer_ffn(xs[g], ffn_norm[i], peer_w_q[i], peer_sub_keys[i], u_all, v_all, i)
    outs = [_rmsnorm(xg, final_norm).reshape(gb, s_len, d) for xg in xs]
    return jnp.concatenate(outs, axis=0)
```

```python
import functools

import numpy as np
import jax
import jax.numpy as jnp
from jax import lax
from jax.experimental import pallas as pl
from jax.experimental.pallas import tpu as pltpu
from jax.experimental.pallas import tpu_sc as plsc

NORM_EPS = 1e-6
ROPE_THETA = 10000.0

A_HEADS = 8
A_HEAD_DIM = 64
IDX_HEADS = 8
IDX_DIM = 64
DSA_TOPK_MAX = 256

B_HEADS = 8
B_HEAD_DIM = 64
MOBA_BLOCK = 256
MOBA_TOPK = 3

C_HEADS = 16
C_NOPE = 64
C_ROPE = 32
C_V = 64
C_Q_RANK = 768
C_KV_RANK = 256

PEER_HEADS = 8
PEER_NKEYS = 128
PEER_QDIM = 256
PEER_TOPK_HALF = 16
PEER_TOPK = 16

BATCH_GROUPS = 4
LANES = 128
VMEM_LIMIT = 56 * 1024 * 1024

F32 = jnp.float32
BF16 = jnp.bfloat16
I32 = jnp.int32
INT_MIN = -(2 ** 31)


def _params(*sem):
    return pltpu.CompilerParams(dimension_semantics=sem, vmem_limit_bytes=VMEM_LIMIT)


def _dot(a, b):
    return jnp.dot(a, b, preferred_element_type=F32)


def _dot_nt(a, b):
    return lax.dot_general(a, b, (((1,), (1,)), ((), ())), preferred_element_type=F32)


def _rope_table_kernel(pos_ref, inv_ref, sign_ref, cos_ref, sin_ref):
    ang = pos_ref[...].astype(F32) * inv_ref[...]
    cos_ref[...] = jnp.cos(ang)
    sin_ref[...] = jnp.sin(ang) * sign_ref[...]


def _rope_tables(pos, dim, tm=1024):
    n = pos.shape[0]
    half = dim // 2
    inv = 1.0 / (ROPE_THETA ** (jnp.arange(0, dim, 2, dtype=F32) / dim))
    reps = LANES // dim
    inv_l = jnp.tile(jnp.concatenate([inv, inv]), reps)[None, :]
    sign_l = jnp.tile(jnp.concatenate([-jnp.ones(half, F32), jnp.ones(half, F32)]), reps)[None, :]
    out = jax.ShapeDtypeStruct((n, LANES), F32)
    return pl.pallas_call(
        _rope_table_kernel,
        grid=(n // tm,),
        in_specs=[pl.BlockSpec((tm, 1), lambda i: (i, 0)),
                  pl.BlockSpec((1, LANES), lambda i: (0, 0)),
                  pl.BlockSpec((1, LANES), lambda i: (0, 0))],
        out_specs=[pl.BlockSpec((tm, LANES), lambda i: (i, 0))] * 2,
        out_shape=[out, out],
        compiler_params=_params("parallel"),
        name="rope_tables",
    )(pos, inv_l, sign_l)


def _rope_epilogue(acc, cos, sin, half):
    wd = acc.shape[1]
    reps = wd // LANES
    if reps > 1:
        cos = jnp.concatenate([cos] * reps, axis=1)
        sin = jnp.concatenate([sin] * reps, axis=1)
    lane = lax.broadcasted_iota(I32, acc.shape, 1)
    first = (lane % (2 * half)) < half
    partner = jnp.where(first, pltpu.roll(acc, wd - half, 1), pltpu.roll(acc, half, 1))
    return acc * cos + partner * sin


def _proj_kernel(*refs, groups, has_gain, emit_h, tab_names):
    it = iter(refs)
    x_ref = next(it)
    g_ref = next(it) if has_gain else None
    w_ref = next(it)
    tabs = {name: (next(it), next(it)) for name in tab_names}
    out_refs = [next(it) for _ in groups]
    h_ref = next(it) if emit_h else None

    x = x_ref[...]
    if has_gain:
        ms = jnp.mean(x * x, axis=-1, keepdims=True)
        h = x * lax.rsqrt(ms + NORM_EPS) * g_ref[...]
    else:
        h = x
    if emit_h:
        h_ref[...] = h
    hb = h.astype(BF16)
    for o_ref, (c0, wd, rope) in zip(out_refs, groups):
        acc = _dot(hb, w_ref[:, c0:c0 + wd])
        if rope is not None:
            cos_ref, sin_ref = tabs[rope]
            acc = _rope_epilogue(acc, cos_ref[...], sin_ref[...], 32 if rope == "r64" else 16)
        o_ref[...] = acc


def _fused_proj(x, gain, w, groups, tables, emit_h=False, tm=256):
    n, k = x.shape
    tab_names = sorted({g[2] for g in groups if g[2] is not None})
    args = [x]
    in_specs = [pl.BlockSpec((tm, k), lambda i: (i, 0))]
    if gain is not None:
        args.append(gain.reshape(1, k).astype(F32))
        in_specs.append(pl.BlockSpec((1, k), lambda i: (0, 0)))
    args.append(w)
    in_specs.append(pl.BlockSpec(w.shape, lambda i: (0, 0)))
    for name in tab_names:
        for t in tables[name]:
            args.append(t)
            in_specs.append(pl.BlockSpec((tm, LANES), lambda i: (i, 0)))
    out_shape = [jax.ShapeDtypeStruct((n, wd), F32) for (_, wd, _) in groups]
    out_specs = [pl.BlockSpec((tm, wd), lambda i: (i, 0)) for (_, wd, _) in groups]
    if emit_h:
        out_shape.append(jax.ShapeDtypeStruct((n, k), F32))
        out_specs.append(pl.BlockSpec((tm, k), lambda i: (i, 0)))
    kern = functools.partial(_proj_kernel, groups=tuple(groups), has_gain=gain is not None,
                             emit_h=emit_h, tab_names=tuple(tab_names))
    return pl.pallas_call(
        kern, grid=(n // tm,), in_specs=in_specs, out_specs=out_specs, out_shape=out_shape,
        compiler_params=_params("parallel"), name="fused_proj",
    )(*args)


def _outproj_kernel(*refs, n_in):
    res_ref = refs[0]
    a_refs = refs[1:1 + n_in]
    w_refs = refs[1 + n_in:1 + 2 * n_in]
    o_ref = refs[-1]
    acc = res_ref[...]
    for a_ref, w_ref in zip(a_refs, w_refs):
        acc = acc + _dot(a_ref[...].astype(BF16), w_ref[...])
    o_ref[...] = acc


def _outproj_residual(res, acts, ws, tm=256):
    n, d = res.shape
    in_specs = [pl.BlockSpec((tm, d), lambda i: (i, 0))]
    in_specs += [pl.BlockSpec((tm, a.shape[1]), lambda i: (i, 0)) for a in acts]
    in_specs += [pl.BlockSpec(w.shape, lambda i: (0, 0)) for w in ws]
    return pl.pallas_call(
        functools.partial(_outproj_kernel, n_in=len(acts)),
        grid=(n // tm,), in_specs=in_specs,
        out_specs=pl.BlockSpec((tm, d), lambda i: (i, 0)),
        out_shape=jax.ShapeDtypeStruct((n, d), F32),
        compiler_params=_params("parallel"), name="outproj_residual",
    )(res, *acts, *ws)


def _rmsnorm_kernel(x_ref, g_ref, o_ref):
    x = x_ref[...]
    ms = jnp.mean(x * x, axis=-1, keepdims=True)
    o_ref[...] = x * lax.rsqrt(ms + NORM_EPS) * g_ref[...]


def _rmsnorm(x, gain, tm=512):
    n, d = x.shape
    return pl.pallas_call(
        _rmsnorm_kernel, grid=(n // tm,),
        in_specs=[pl.BlockSpec((tm, d), lambda i: (i, 0)), pl.BlockSpec((1, d), lambda i: (0, 0))],
        out_specs=pl.BlockSpec((tm, d), lambda i: (i, 0)),
        out_shape=jax.ShapeDtypeStruct((n, d), F32),
        compiler_params=_params("parallel"), name="final_rmsnorm",
    )(x, gain.reshape(1, d))


def _softmax_pv(s, mask, v_bf):
    s = jnp.where(mask, s, -jnp.inf)
    m = jnp.max(s, axis=-1, keepdims=True)
    p = jnp.exp(s - m)
    l = jnp.sum(p, axis=-1, keepdims=True)
    return _dot(p.astype(BF16), v_bf) / l


def _dsa_kernel(iq_ref, qa_ref, kaik_ref, va_ref, iw_ref, o_ref, *, n_sel, tq, s_len):
    i = pl.program_id(1)
    kaik = kaik_ref[...]
    ka = kaik[:, 0:A_HEAD_DIM].astype(BF16)
    ik = kaik[:, A_HEAD_DIM:A_HEAD_DIM + IDX_DIM].astype(BF16)
    va = va_ref[:, 0:A_HEAD_DIM].astype(BF16)
    iw = iw_ref[:, A_HEAD_DIM:A_HEAD_DIM + IDX_HEADS]
    iq = iq_ref[...]

    score = jnp.zeros((tq, s_len), F32)
    for h in range(IDX_HEADS):
        lg = _dot_nt(iq[:, h * IDX_DIM:(h + 1) * IDX_DIM].astype(BF16), ik)
        score = score + jnp.maximum(lg, 0.0) * iw[:, h:h + 1]
    w_scale = (IDX_HEADS ** -0.5) * (IDX_DIM ** -0.5)
    score = score * w_scale + 0.0

    key_pos = lax.broadcasted_iota(I32, (tq, s_len), 1)
    q_pos = i * tq + lax.broadcasted_iota(I32, (tq, s_len), 0)
    causal = key_pos <= q_pos

    bits = pltpu.bitcast(score, I32)
    key = jnp.where(bits < 0, bits ^ 0x7FFFFFFF, bits)
    key = jnp.where(causal, key, INT_MIN)

    def count_ge(cand):
        return jnp.sum(jnp.where(key >= cand, 1.0, 0.0), axis=-1, keepdims=True)

    lo = jnp.where(count_ge(jnp.zeros((tq, 1), I32)) >= n_sel, 0, INT_MIN).astype(I32)

    def bisect(t, lo):
        cand = lo + jnp.left_shift(jnp.int32(1), 30 - t)
        return jnp.where(count_ge(cand) >= n_sel, cand, lo)

    thr = lax.fori_loop(0, 31, bisect, lo)

    gt = key > thr
    eq = key == thr
    need = n_sel - jnp.sum(jnp.where(gt, 1.0, 0.0), axis=-1, keepdims=True)
    r = lax.broadcasted_iota(I32, (LANES, LANES), 0)
    c = lax.broadcasted_iota(I32, (LANES, LANES), 1)
    tri = jnp.where(r < c, 1.0, 0.0).astype(BF16)
    eq_bf = jnp.where(eq, 1.0, 0.0).astype(BF16)
    carry = jnp.zeros((tq, 1), F32)
    pref = []
    for cb in range(s_len // LANES):
        blk = eq_bf[:, cb * LANES:(cb + 1) * LANES]
        pref.append(_dot(blk, tri) + carry)
        carry = carry + jnp.sum(blk.astype(F32), axis=-1, keepdims=True)
    prefix = jnp.concatenate(pref, axis=1)
    sel = (gt | (eq & (prefix < need))) & causal

    a_scale = A_HEAD_DIM ** -0.5
    qa = qa_ref[...]
    outs = []
    for h in range(A_HEADS):
        sc = _dot_nt(qa[:, h * A_HEAD_DIM:(h + 1) * A_HEAD_DIM].astype(BF16), ka) * a_scale
        outs.append(_softmax_pv(sc, sel, va))
    o_ref[...] = jnp.concatenate(outs, axis=1)


def _dsa_attention(iq, qa, kaik, vaiw, batch, s_len, tq=128):
    n = iq.shape[0]
    nq = s_len // tq
    n_sel = min(DSA_TOPK_MAX, s_len // 4)
    hd = A_HEADS * A_HEAD_DIM
    kern = functools.partial(_dsa_kernel, n_sel=n_sel, tq=tq, s_len=s_len)
    return pl.pallas_call(
        kern, grid=(batch, nq),
        in_specs=[pl.BlockSpec((tq, IDX_HEADS * IDX_DIM), lambda b, i: (b * nq + i, 0)),
                  pl.BlockSpec((tq, hd), lambda b, i: (b * nq + i, 0)),
                  pl.BlockSpec((s_len, LANES), lambda b, i: (b, 0)),
                  pl.BlockSpec((s_len, LANES), lambda b, i: (b, 0)),
                  pl.BlockSpec((tq, LANES), lambda b, i: (b * nq + i, 0))],
        out_specs=pl.BlockSpec((tq, hd), lambda b, i: (b * nq + i, 0)),
        out_shape=jax.ShapeDtypeStruct((n, hd), F32),
        compiler_params=_params("parallel", "arbitrary"), name="dsa_attention",
    )(iq, qa, kaik, vaiw, vaiw)


def _moba_kernel(q_ref, k_ref, v_ref, o_ref, *, n_top, s_len):
    i = pl.program_id(2)
    tq = MOBA_BLOCK
    nb = s_len // MOBA_BLOCK
    d = B_HEAD_DIM
    scale = d ** -0.5
    key_pos = lax.broadcasted_iota(I32, (tq, s_len), 1)
    q_pos = i * tq + lax.broadcasted_iota(I32, (tq, s_len), 0)
    own_mask = (key_pos >= i * MOBA_BLOCK) & (key_pos <= q_pos)
    blk_lane = lax.broadcasted_iota(I32, (tq, nb), 1)
    expand = jnp.where(lax.broadcasted_iota(I32, (nb, s_len), 1) // MOBA_BLOCK
                       == lax.broadcasted_iota(I32, (nb, s_len), 0), 1.0, 0.0).astype(BF16)
    outs = []
    for hh in range(LANES // d):
        q = q_ref[:, hh * d:(hh + 1) * d].astype(BF16)
        k = k_ref[:, hh * d:(hh + 1) * d]
        v = v_ref[:, hh * d:(hh + 1) * d].astype(BF16)
        k_mean = jnp.sum(k.reshape(nb, MOBA_BLOCK, d), axis=1) / MOBA_BLOCK
        gate = _dot_nt(q, k_mean.astype(BF16))
        gate = jnp.where(blk_lane < i, gate, -jnp.inf)
        rank = jnp.zeros((tq, nb), F32)
        for b2 in range(nb):
            gb = gate[:, b2:b2 + 1]
            beats = (gb > gate) | ((gb == gate) & (b2 < blk_lane))
            rank = rank + jnp.where(beats, 1.0, 0.0)
        sel = jnp.where((rank < n_top) & (blk_lane < i), 1.0, 0.0).astype(BF16)
        selk = _dot(sel, expand)
        mask = (selk > 0.5) | own_mask
        s = _dot_nt(q, k.astype(BF16)) * scale
        outs.append(_softmax_pv(s, mask, v))
    o_ref[...] = jnp.concatenate(outs, axis=1)


def _moba_attention(qb, kb, vb, batch, s_len):
    n = qb.shape[0]
    nb = s_len // MOBA_BLOCK
    n_top = min(MOBA_TOPK, nb - 1)
    hp = B_HEADS * B_HEAD_DIM // LANES
    kern = functools.partial(_moba_kernel, n_top=n_top, s_len=s_len)
    return pl.pallas_call(
        kern, grid=(batch, hp, nb),
        in_specs=[pl.BlockSpec((MOBA_BLOCK, LANES), lambda b, g, i: (b * nb + i, g)),
                  pl.BlockSpec((s_len, LANES), lambda b, g, i: (b, g)),
                  pl.BlockSpec((s_len, LANES), lambda b, g, i: (b, g))],
        out_specs=pl.BlockSpec((MOBA_BLOCK, LANES), lambda b, g, i: (b * nb + i, g)),
        out_shape=jax.ShapeDtypeStruct((n, B_HEADS * B_HEAD_DIM), F32),
        compiler_params=_params("parallel", "parallel", "arbitrary"), name="moba_attention",
    )(qb, kb, vb)


MLA_GROUP = 4


def _mla_kernel(qn_ref, qr_ref, kn_ref, kr_ref, v_ref, o_ref, *, tq, s_len):
    i = pl.program_id(2)
    scale = (C_NOPE + C_ROPE) ** -0.5
    key_pos = lax.broadcasted_iota(I32, (tq, s_len), 1)
    q_pos = i * tq + lax.broadcasted_iota(I32, (tq, s_len), 0)
    causal = key_pos <= q_pos
    kr = kr_ref[:, 0:C_ROPE].astype(BF16)
    outs = []
    for hh in range(MLA_GROUP):
        qn = qn_ref[:, hh * C_NOPE:(hh + 1) * C_NOPE].astype(BF16)
        qr = qr_ref[:, hh * C_ROPE:(hh + 1) * C_ROPE].astype(BF16)
        kn = kn_ref[:, hh * C_NOPE:(hh + 1) * C_NOPE].astype(BF16)
        v = v_ref[:, hh * C_V:(hh + 1) * C_V].astype(BF16)
        s = (_dot_nt(qn, kn) + _dot_nt(qr, kr)) * scale
        outs.append(_softmax_pv(s, causal, v))
    o_ref[...] = jnp.concatenate(outs, axis=1)


def _mla_attention(qn, qr, kn, kr, v, batch, s_len, tq=256):
    n = qn.shape[0]
    nq = s_len // tq
    ng = C_HEADS // MLA_GROUP
    kern = functools.partial(_mla_kernel, tq=tq, s_len=s_len)
    return pl.pallas_call(
        kern, grid=(batch, ng, nq),
        in_specs=[pl.BlockSpec((tq, MLA_GROUP * C_NOPE), lambda b, g, i: (b * nq + i, g)),
                  pl.BlockSpec((tq, MLA_GROUP * C_ROPE), lambda b, g, i: (b * nq + i, g)),
                  pl.BlockSpec((s_len, MLA_GROUP * C_NOPE), lambda b, g, i: (b, g)),
                  pl.BlockSpec((s_len, LANES), lambda b, g, i: (b, 0)),
                  pl.BlockSpec((s_len, MLA_GROUP * C_V), lambda b, g, i: (b, g))],
        out_specs=pl.BlockSpec((tq, MLA_GROUP * C_V), lambda b, g, i: (b * nq + i, g)),
        out_shape=jax.ShapeDtypeStruct((n, C_HEADS * C_V), F32),
        compiler_params=_params("parallel", "parallel", "arbitrary"), name="mla_attention",
    )(qn, qr, kn, kr, v)


def _topk_rows(sc, k):
    r = sc.shape[0]
    riota = lax.broadcasted_iota(I32, sc.shape, 0)
    vals, rows = [], []
    for _ in range(k):
        m = jnp.max(sc, axis=0, keepdims=True)
        pos = jnp.min(jnp.where(sc == m, riota, r), axis=0, keepdims=True)
        sc = jnp.where(riota == pos, -jnp.inf, sc)
        vals.append(m)
        rows.append(pos)
    return jnp.concatenate(vals, axis=0), jnp.concatenate(rows, axis=0)


def _peer_candidates():
    k = PEER_TOPK_HALF
    rows = [(0, b) for b in range(k)]
    rows += [(a, b) for a in range(1, 8) for b in range(8)]
    rows += [(a, 0) for a in range(8, k)]
    return np.array([a * k + b if (a + 1) * (b + 1) <= PEER_TOPK else -1 for a, b in rows], np.int32)


def _select_rows(tab, row):
    riota = lax.broadcasted_iota(I32, tab.shape, 0)
    return jnp.sum(jnp.where(riota == row, tab, 0), axis=0, keepdims=True)


def _peer_topk_kernel(q_ref, keys_ref, pos_ref, e_ref, g_ref, *, idx_offset):
    kh = PEER_TOPK_HALF
    pos_tab = pos_ref[...]
    big = kh * kh
    e_rows, g_rows = [], []
    for h in range(PEER_HEADS):
        half_v, half_i = [], []
        for p in range(2):
            c0 = (h * 2 + p) * PEER_NKEYS
            qh = q_ref[:, c0:c0 + PEER_NKEYS].astype(BF16)
            sc = _dot_nt(keys_ref[h, p], qh)
            v, ix = _topk_rows(sc, kh)
            half_v.append(v)
            half_i.append(ix)
        v0, v1 = half_v
        cand = jnp.concatenate(
            [v0[0:1] + v1] + [v0[a:a + 1] + v1[0:8] for a in range(1, 8)] + [v0[8:kh] + v1[0:1]], axis=0)
        cand = jnp.where(pos_tab >= 0, cand, -jnp.inf)
        pos_key = jnp.where(pos_tab >= 0, pos_tab, big)
        tops = []
        for _ in range(PEER_TOPK):
            m = jnp.max(cand, axis=0, keepdims=True)
            pos = jnp.min(jnp.where(cand == m, pos_key, big), axis=0, keepdims=True)
            cand = jnp.where(pos_key == pos, -jnp.inf, cand)
            tops.append(m)
            e_rows.append(_select_rows(half_i[0], pos // kh) * PEER_NKEYS
                          + _select_rows(half_i[1], pos % kh) + idx_offset)
        top_s = jnp.concatenate(tops, axis=0)
        ex = jnp.exp(top_s - top_s[0:1, :])
        g_rows.append(ex / jnp.sum(ex, axis=0, keepdims=True))
    e_ref[...] = jnp.concatenate(e_rows, axis=0).T
    g_ref[...] = jnp.concatenate(g_rows, axis=0).T


def _peer_topk(q, keys_bf, idx_offset, t=128):
    n = q.shape[0]
    pos_tab = jnp.asarray(np.tile(_peer_candidates()[:, None], (1, t)))
    out = pl.BlockSpec((t, PEER_SEL), lambda i: (i, 0))
    return pl.pallas_call(
        functools.partial(_peer_topk_kernel, idx_offset=idx_offset), grid=(n // t,),
        in_specs=[pl.BlockSpec((t, PEER_HEADS * PEER_QDIM), lambda i: (i, 0)),
                  pl.BlockSpec(keys_bf.shape, lambda i: (0, 0, 0, 0)),
                  pl.BlockSpec(pos_tab.shape, lambda i: (0, 0))],
        out_specs=[out, out],
        out_shape=[jax.ShapeDtypeStruct((n, PEER_SEL), I32), jax.ShapeDtypeStruct((n, PEER_SEL), F32)],
        compiler_params=_params("parallel"), name="peer_topk",
    )(q, keys_bf, pos_tab)


PEER_SEL = PEER_HEADS * PEER_TOPK

SC_CORES = 2
SC_SUBCORES = 16
SC_LANES = 16
SC_WORKERS = SC_CORES * SC_SUBCORES
SC_ROWS = 32
SC_TOK = 8
SC_CHUNKS = SC_TOK * PEER_SEL // SC_ROWS
SC_VCOLS = 256
SC_UNROLL = 1


def _sc_mesh():
    return plsc.VectorSubcoreMesh(core_axis_name="c", subcore_axis_name="s",
                                  num_cores=SC_CORES, num_subcores=SC_SUBCORES)


def _pack_table(t):
    half = t.shape[1] // 2
    b = lax.bitcast_convert_type(t.astype(BF16), jnp.uint16).astype(jnp.uint32)
    return lax.bitcast_convert_type(b[:, :half] | (b[:, half:] << 16), I32)


def _unpack_pair(words):
    lo = lax.bitcast_convert_type(jnp.left_shift(words, 16), F32)
    hi = lax.bitcast_convert_type(jnp.bitwise_and(words, jnp.int32(-65536)), F32)
    return lo, hi


def _sc_chunk_pipeline(gather, compute):
    gather(0, 0).start()

    @pl.loop(0, SC_CHUNKS, step=2)
    def _(q):
        gather(q + 1, 1).start()
        gather(q, 0).wait()
        compute(q, 0)

        @pl.when(q + 2 < SC_CHUNKS)
        def _():
            gather(q + 2, 0).start()

        gather(q + 1, 1).wait()
        compute(q + 1, 1)


def _sc_u_body(idx_hbm, h_hbm, u_hbm, act_hbm, idx_v, h_v, rows_v, act_v, sem0, sem1, *, tpw, d):
    tok_base = (lax.axis_index("s") * SC_CORES + lax.axis_index("c")) * tpw
    sems = (sem0, sem1)

    def gather(q, b):
        return pltpu.make_async_copy(u_hbm.at[idx_v.at[pl.ds(q * SC_ROWS, SC_ROWS)]], rows_v.at[b], sems[b])

    def compute(q, b):
        tok = q // (PEER_SEL // SC_ROWS)
        lane = lax.iota(I32, SC_LANES)
        for g in range(SC_ROWS // SC_LANES):
            def body(c, accs, g=g):
                x_lo = h_v[tok, pl.ds(c * SC_LANES, SC_LANES)]
                x_hi = h_v[tok, pl.ds(d // 2 + c * SC_LANES, SC_LANES)]
                out = []
                for r, a in enumerate(accs):
                    lo, hi = _unpack_pair(rows_v[b, g * SC_LANES + r, pl.ds(c * SC_LANES, SC_LANES)])
                    out.append(a + lo * x_lo + hi * x_hi)
                return tuple(out)
            accs = lax.fori_loop(0, d // (2 * SC_LANES), body,
                                 tuple(jnp.zeros((SC_LANES,), F32) for _ in range(SC_LANES)),
                                 unroll=SC_UNROLL)
            out = jnp.zeros((SC_LANES,), F32)
            for r in range(SC_LANES):
                out = jnp.where(lane == r, jnp.sum(accs[r]), out)
            act_v[pl.ds(q * SC_ROWS + g * SC_LANES, SC_LANES)] = out

    @pl.loop(0, tpw // SC_TOK)
    def _(blk):
        tok0 = tok_base + blk * SC_TOK
        pltpu.sync_copy(idx_hbm.at[pl.ds(tok0 * PEER_SEL, SC_TOK * PEER_SEL)], idx_v)
        pltpu.sync_copy(h_hbm.at[pl.ds(tok0, SC_TOK)], h_v)
        _sc_chunk_pipeline(gather, compute)
        pltpu.sync_copy(act_v, act_hbm.at[pl.ds(tok0 * PEER_SEL, SC_TOK * PEER_SEL)])


def _sc_peer_u(idx_flat, h, u_tab):
    n, d = h.shape
    assert u_tab.shape[1] * 2 == d and u_tab.dtype == I32
    assert n % (SC_WORKERS * SC_TOK) == 0 and d % SC_VCOLS == 0
    body = functools.partial(_sc_u_body, tpw=n // SC_WORKERS, d=d)
    return pl.kernel(
        body,
        out_type=jax.ShapeDtypeStruct((n * PEER_SEL,), F32),
        mesh=_sc_mesh(),
        scratch_types=[pltpu.VMEM((SC_TOK * PEER_SEL,), I32),
                       pltpu.VMEM((SC_TOK, d), F32),
                       pltpu.VMEM((2, SC_ROWS, d // 2), I32),
                       pltpu.VMEM((SC_TOK * PEER_SEL,), F32),
                       pltpu.SemaphoreType.DMA, pltpu.SemaphoreType.DMA],
        compiler_params=pltpu.CompilerParams(needs_layout_passes=False),
        name="sc_peer_u",
    )(idx_flat, h, u_tab)


def _sc_v_body(idx_hbm, w_hbm, x_hbm, v_hbm, o_hbm, idx_v, w_v, out_v, rows_v, sem0, sem1, *, tpw, d):
    tok_base = (lax.axis_index("s") * SC_CORES + lax.axis_index("c")) * tpw
    sems = (sem0, sem1)
    half = d // 2
    nword = SC_VCOLS // (2 * SC_LANES)

    def gather(q, b):
        return pltpu.make_async_copy(v_hbm.at[idx_v.at[pl.ds(q * SC_ROWS, SC_ROWS)]], rows_v.at[b], sems[b])

    def compute(q, b):
        tok = q // (PEER_SEL // SC_ROWS)
        for cp in range(d // SC_VCOLS):
            w0 = cp * nword * SC_LANES
            cols = [half * part + w0 + a * SC_LANES for a in range(nword) for part in range(2)]

            def body(r, accs, w0=w0):
                wv = plsc.load_gather(w_v, [jnp.full((SC_LANES,), q * SC_ROWS + r, I32)])
                out = []
                for a in range(nword):
                    lo, hi = _unpack_pair(rows_v[b, r, pl.ds(w0 + a * SC_LANES, SC_LANES)])
                    out += [accs[2 * a] + wv * lo, accs[2 * a + 1] + wv * hi]
                return tuple(out)

            accs = lax.fori_loop(0, SC_ROWS, body,
                                 tuple(out_v[tok, pl.ds(c, SC_LANES)] for c in cols))
            for c, acc in zip(cols, accs):
                out_v[tok, pl.ds(c, SC_LANES)] = acc

    @pl.loop(0, tpw // SC_TOK)
    def _(blk):
        tok0 = tok_base + blk * SC_TOK
        pltpu.sync_copy(idx_hbm.at[pl.ds(tok0 * PEER_SEL, SC_TOK * PEER_SEL)], idx_v)
        pltpu.sync_copy(w_hbm.at[pl.ds(tok0 * PEER_SEL, SC_TOK * PEER_SEL)], w_v)
        pltpu.sync_copy(x_hbm.at[pl.ds(tok0, SC_TOK)], out_v)
        _sc_chunk_pipeline(gather, compute)
        pltpu.sync_copy(out_v, o_hbm.at[pl.ds(tok0, SC_TOK)])


def _sc_peer_v(idx_flat, w_flat, x, v_tab):
    n, d = x.shape
    assert v_tab.shape[1] * 2 == d and v_tab.dtype == I32
    assert n % (SC_WORKERS * SC_TOK) == 0 and d % SC_VCOLS == 0
    body = functools.partial(_sc_v_body, tpw=n // SC_WORKERS, d=d)
    return pl.kernel(
        body,
        out_type=jax.ShapeDtypeStruct((n, d), F32),
        mesh=_sc_mesh(),
        scratch_types=[pltpu.VMEM((SC_TOK * PEER_SEL,), I32),
                       pltpu.VMEM((SC_TOK * PEER_SEL,), F32),
                       pltpu.VMEM((SC_TOK, d), F32),
                       pltpu.VMEM((2, SC_ROWS, d // 2), I32),
                       pltpu.SemaphoreType.DMA, pltpu.SemaphoreType.DMA],
        compiler_params=pltpu.CompilerParams(needs_layout_passes=False),
        name="sc_peer_v",
    )(idx_flat, w_flat, x, v_tab)


def _peer_gate_kernel(g_ref, act_ref, w_ref):
    act = act_ref[...]
    w_ref[...] = g_ref[...] * (0.5 * act * (1.0 + lax.erf(act * (2.0 ** -0.5))))


def _peer_gate(g, act, tm=2048):
    n = g.shape[0]
    spec = pl.BlockSpec((tm, PEER_SEL), lambda i: (i, 0))
    return pl.pallas_call(
        _peer_gate_kernel, grid=(n // tm,), in_specs=[spec, spec], out_specs=spec,
        out_shape=jax.ShapeDtypeStruct((n, PEER_SEL), F32),
        compiler_params=_params("parallel"), name="peer_gate",
    )(g, act)


def _pad_cols(w, width):
    return jnp.pad(w, ((0, 0), (0, width - w.shape[1])))


def _even_weights(w_in):
    hd = A_HEADS * A_HEAD_DIM
    cols = np.cumsum([0, hd, A_HEAD_DIM, A_HEAD_DIM, IDX_HEADS * IDX_DIM, IDX_DIM, IDX_HEADS,
                      B_HEADS * B_HEAD_DIM, B_HEADS * B_HEAD_DIM, B_HEADS * B_HEAD_DIM])
    qa, ka, va, iq, ik, iw, qb, kb, vb = [w_in[:, cols[j]:cols[j + 1]] for j in range(9)]
    vaiw = _pad_cols(jnp.concatenate([va, iw], axis=1), LANES)
    w = jnp.concatenate([qa, iq, qb, kb, vb, ka, ik, vaiw], axis=1).astype(BF16)
    groups = [(0, 512, "r64"), (512, 512, "r64"), (1024, 512, "r64"), (1536, 512, "r64"),
              (2048, 512, None), (2560, LANES, "r64"), (2560 + LANES, LANES, None)]
    return w, groups


def _even_mixer(x, gain, w_in, w_out, tables, batch, s_len):
    w, groups = _even_weights(w_in)
    qa, iq, qb, kb, vb, kaik, vaiw = _fused_proj(x, gain, w, groups, tables)
    out_a = _dsa_attention(iq, qa, kaik, vaiw, batch, s_len)
    out_b = _moba_attention(qb, kb, vb, batch, s_len)
    hd = A_HEADS * A_HEAD_DIM
    w_out = w_out.astype(BF16)
    return _outproj_residual(x, [out_a, out_b], [w_out[:hd], w_out[hd:]])


def _mla_mixer(x, gain, w_in, q_norm, kv_norm, w_uq, w_ukv, w_out, tables, batch, s_len):
    w1 = _pad_cols(w_in, C_Q_RANK + C_KV_RANK + LANES).astype(BF16)
    groups1 = [(0, C_Q_RANK, None), (C_Q_RANK, C_KV_RANK, None), (C_Q_RANK + C_KV_RANK, LANES, "r32")]
    cq, ckv, kr = _fused_proj(x, gain, w1, groups1, tables)
    qd = C_NOPE + C_ROPE
    nope_cols = np.concatenate([np.arange(h * qd, h * qd + C_NOPE) for h in range(C_HEADS)])
    rope_cols = np.concatenate([np.arange(h * qd + C_NOPE, (h + 1) * qd) for h in range(C_HEADS)])
    w2 = jnp.concatenate([w_uq[:, nope_cols], w_uq[:, rope_cols]], axis=1).astype(BF16)
    groups2 = [(0, C_HEADS * C_NOPE, None), (C_HEADS * C_NOPE, C_HEADS * C_ROPE, "r32")]
    qn, qr = _fused_proj(cq, q_norm, w2, groups2, tables)
    kd = C_NOPE + C_V
    kn_cols = np.concatenate([np.arange(h * kd, h * kd + C_NOPE) for h in range(C_HEADS)])
    v_cols = np.concatenate([np.arange(h * kd + C_NOPE, (h + 1) * kd) for h in range(C_HEADS)])
    w3 = jnp.concatenate([w_ukv[:, kn_cols], w_ukv[:, v_cols]], axis=1).astype(BF16)
    groups3 = [(0, C_HEADS * C_NOPE, None), (C_HEADS * C_NOPE, C_HEADS * C_V, None)]
    kn, v = _fused_proj(ckv, kv_norm, w3, groups3, tables)
    o = _mla_attention(qn, qr, kn, kr, v, batch, s_len)
    return _outproj_residual(x, [o], [w_out.astype(BF16)])


def _peer_ffn(x, gain, w_q, sub_keys, u_all, v_all, layer):
    n, d = x.shape
    n_experts = PEER_NKEYS * PEER_NKEYS
    q, h = _fused_proj(x, gain, w_q.astype(BF16), [(0, PEER_HEADS * PEER_QDIM, None)], {}, emit_h=True)
    idx, g = _peer_topk(q, sub_keys.astype(BF16), layer * n_experts)
    idx_flat = idx.reshape(n * PEER_SEL)
    act = _sc_peer_u(idx_flat, h, u_all)
    w = _peer_gate(g, act.reshape(n, PEER_SEL))
    return _sc_peer_v(idx_flat, w.reshape(n * PEER_SEL), x, v_all)


def kernel(x, positions, attn_norm, ffn_norm, final_norm, hyb_w_in, hyb_w_out, mla_w_in, mla_q_norm,
           mla_kv_norm, mla_w_uq, mla_w_ukv, mla_w_out, peer_w_q, peer_sub_keys, peer_u, peer_v):
    batch, s_len, d = x.shape
    depth = attn_norm.shape[0]
    u_all = _pack_table(peer_u.reshape(-1, d))
    v_all = _pack_table(peer_v.reshape(-1, d))
    gb = batch // BATCH_GROUPS
    gn = gb * s_len
    xs, tabs = [], []
    for g in range(BATCH_GROUPS):
        pos = positions[g * gb:(g + 1) * gb].reshape(gn, 1).astype(I32)
        tabs.append({"r64": _rope_tables(pos, A_HEAD_DIM), "r32": _rope_tables(pos, C_ROPE)})
        xs.append(x[g * gb:(g + 1) * gb].reshape(gn, d))
    for i in range(depth):
        j = i // 2
        for g in range(BATCH_GROUPS):
            if i % 2 == 0:
                xs[g] = _even_mixer(xs[g], attn_norm[i], hyb_w_in[j], hyb_w_out[j], tabs[g], gb, s_len)
            else:
                xs[g] = _mla_mixer(xs[g], attn_norm[i], mla_w_in[j], mla_q_norm[j], mla_kv_norm[j],
                                   mla_w_uq[j], mla_w_ukv[j], mla_w_out[j], tabs[g], gb, s_len)
            xs[g] = _peer_ffn(xs[g], ffn_norm[i], peer_w_q[i], peer_sub_keys[i], u_all, v_all, i)
    outs = [_rmsnorm(xg, final_norm).reshape(gb, s_len, d) for xg in xs]
    return jnp.concatenate(outs, axis=0)
```

```python
import functools

import numpy as np
import jax
import jax.numpy as jnp
from jax import lax
from jax.experimental import pallas as pl
from jax.experimental.pallas import tpu as pltpu
from jax.experimental.pallas import tpu_sc as plsc

NORM_EPS = 1e-6
ROPE_THETA = 10000.0

A_HEADS = 8
A_HEAD_DIM = 64
IDX_HEADS = 8
IDX_DIM = 64
DSA_TOPK_MAX = 256

B_HEADS = 8
B_HEAD_DIM = 64
MOBA_BLOCK = 256
MOBA_TOPK = 3

C_HEADS = 16
C_NOPE = 64
C_ROPE = 32
C_V = 64
C_Q_RANK = 768
C_KV_RANK = 256

PEER_HEADS = 8
PEER_NKEYS = 128
PEER_QDIM = 256
PEER_TOPK_HALF = 16
PEER_TOPK = 16

BATCH_GROUPS = 8
ATTN_MAX_SPLIT = 4
LANES = 128
VMEM_LIMIT = 56 * 1024 * 1024

F32 = jnp.float32
BF16 = jnp.bfloat16
I32 = jnp.int32
INT_MIN = -(2 ** 31)


def _params(*sem):
    return pltpu.CompilerParams(dimension_semantics=sem, vmem_limit_bytes=VMEM_LIMIT)


def _dot(a, b):
    return jnp.dot(a, b, preferred_element_type=F32)


def _dot_nt(a, b):
    return lax.dot_general(a, b, (((1,), (1,)), ((), ())), preferred_element_type=F32)


def _rope_table_kernel(pos_ref, inv_ref, sign_ref, cos_ref, sin_ref):
    ang = pos_ref[...].astype(F32) * inv_ref[...]
    cos_ref[...] = jnp.cos(ang)
    sin_ref[...] = jnp.sin(ang) * sign_ref[...]


def _rope_tables(pos, dim, tm=1024):
    n = pos.shape[0]
    half = dim // 2
    inv = 1.0 / (ROPE_THETA ** (jnp.arange(0, dim, 2, dtype=F32) / dim))
    reps = LANES // dim
    inv_l = jnp.tile(jnp.concatenate([inv, inv]), reps)[None, :]
    sign_l = jnp.tile(jnp.concatenate([-jnp.ones(half, F32), jnp.ones(half, F32)]), reps)[None, :]
    out = jax.ShapeDtypeStruct((n, LANES), F32)
    return pl.pallas_call(
        _rope_table_kernel,
        grid=(n // tm,),
        in_specs=[pl.BlockSpec((tm, 1), lambda i: (i, 0)),
                  pl.BlockSpec((1, LANES), lambda i: (0, 0)),
                  pl.BlockSpec((1, LANES), lambda i: (0, 0))],
        out_specs=[pl.BlockSpec((tm, LANES), lambda i: (i, 0))] * 2,
        out_shape=[out, out],
        compiler_params=_params("parallel"),
        name="rope_tables",
    )(pos, inv_l, sign_l)


def _rope_epilogue(acc, cos, sin, half):
    wd = acc.shape[1]
    reps = wd // LANES
    if reps > 1:
        cos = jnp.concatenate([cos] * reps, axis=1)
        sin = jnp.concatenate([sin] * reps, axis=1)
    lane = lax.broadcasted_iota(I32, acc.shape, 1)
    first = (lane % (2 * half)) < half
    partner = jnp.where(first, pltpu.roll(acc, wd - half, 1), pltpu.roll(acc, half, 1))
    return acc * cos + partner * sin


def _proj_kernel(*refs, groups, has_gain, emit_h, tab_names):
    it = iter(refs)
    x_ref = next(it)
    g_ref = next(it) if has_gain else None
    w_ref = next(it)
    tabs = {name: (next(it), next(it)) for name in tab_names}
    out_refs = [next(it) for _ in groups]
    h_ref = next(it) if emit_h else None

    x = x_ref[...]
    if has_gain:
        ms = jnp.mean(x * x, axis=-1, keepdims=True)
        h = x * lax.rsqrt(ms + NORM_EPS) * g_ref[...]
    else:
        h = x
    if emit_h:
        h_ref[...] = h
    hb = h.astype(BF16)
    for o_ref, (c0, wd, rope) in zip(out_refs, groups):
        acc = _dot(hb, w_ref[:, c0:c0 + wd])
        if rope is not None:
            cos_ref, sin_ref = tabs[rope]
            acc = _rope_epilogue(acc, cos_ref[...], sin_ref[...], 32 if rope == "r64" else 16)
        o_ref[...] = acc


def _fused_proj(x, gain, w, groups, tables, emit_h=False, tm=256):
    n, k = x.shape
    tab_names = sorted({g[2] for g in groups if g[2] is not None})
    args = [x]
    in_specs = [pl.BlockSpec((tm, k), lambda i: (i, 0))]
    if gain is not None:
        args.append(gain.reshape(1, k).astype(F32))
        in_specs.append(pl.BlockSpec((1, k), lambda i: (0, 0)))
    args.append(w)
    in_specs.append(pl.BlockSpec(w.shape, lambda i: (0, 0)))
    for name in tab_names:
        for t in tables[name]:
            args.append(t)
            in_specs.append(pl.BlockSpec((tm, LANES), lambda i: (i, 0)))
    out_shape = [jax.ShapeDtypeStruct((n, wd), F32) for (_, wd, _) in groups]
    out_specs = [pl.BlockSpec((tm, wd), lambda i: (i, 0)) for (_, wd, _) in groups]
    if emit_h:
        out_shape.append(jax.ShapeDtypeStruct((n, k), F32))
        out_specs.append(pl.BlockSpec((tm, k), lambda i: (i, 0)))
    kern = functools.partial(_proj_kernel, groups=tuple(groups), has_gain=gain is not None,
                             emit_h=emit_h, tab_names=tuple(tab_names))
    return pl.pallas_call(
        kern, grid=(n // tm,), in_specs=in_specs, out_specs=out_specs, out_shape=out_shape,
        compiler_params=_params("parallel"), name="fused_proj",
    )(*args)


def _outproj_kernel(*refs, n_in):
    res_ref = refs[0]
    a_refs = refs[1:1 + n_in]
    w_refs = refs[1 + n_in:1 + 2 * n_in]
    o_ref = refs[-1]
    acc = res_ref[...]
    for a_ref, w_ref in zip(a_refs, w_refs):
        acc = acc + _dot(a_ref[...].astype(BF16), w_ref[...])
    o_ref[...] = acc


def _outproj_residual(res, acts, ws, tm=256):
    n, d = res.shape
    in_specs = [pl.BlockSpec((tm, d), lambda i: (i, 0))]
    in_specs += [pl.BlockSpec((tm, a.shape[1]), lambda i: (i, 0)) for a in acts]
    in_specs += [pl.BlockSpec(w.shape, lambda i: (0, 0)) for w in ws]
    return pl.pallas_call(
        functools.partial(_outproj_kernel, n_in=len(acts)),
        grid=(n // tm,), in_specs=in_specs,
        out_specs=pl.BlockSpec((tm, d), lambda i: (i, 0)),
        out_shape=jax.ShapeDtypeStruct((n, d), F32),
        compiler_params=_params("parallel"), name="outproj_residual",
    )(res, *acts, *ws)


def _rmsnorm_kernel(x_ref, g_ref, o_ref):
    x = x_ref[...]
    ms = jnp.mean(x * x, axis=-1, keepdims=True)
    o_ref[...] = x * lax.rsqrt(ms + NORM_EPS) * g_ref[...]


def _rmsnorm(x, gain, tm=512):
    n, d = x.shape
    return pl.pallas_call(
        _rmsnorm_kernel, grid=(n // tm,),
        in_specs=[pl.BlockSpec((tm, d), lambda i: (i, 0)), pl.BlockSpec((1, d), lambda i: (0, 0))],
        out_specs=pl.BlockSpec((tm, d), lambda i: (i, 0)),
        out_shape=jax.ShapeDtypeStruct((n, d), F32),
        compiler_params=_params("parallel"), name="final_rmsnorm",
    )(x, gain.reshape(1, d))


def _softmax_pv(s, mask, v_bf):
    s = jnp.where(mask, s, -jnp.inf)
    m = jnp.max(s, axis=-1, keepdims=True)
    p = jnp.exp(s - m)
    l = jnp.sum(p, axis=-1, keepdims=True)
    return _dot(p.astype(BF16), v_bf) / l


def _dsa_kernel(iq_ref, qa_ref, kaik_ref, va_ref, iw_ref, o_ref, *, n_sel, tq, s_len, q_off):
    i = pl.program_id(1) + q_off
    kaik = kaik_ref[...]
    ka = kaik[:, 0:A_HEAD_DIM].astype(BF16)
    ik = kaik[:, A_HEAD_DIM:A_HEAD_DIM + IDX_DIM].astype(BF16)
    va = va_ref[:, 0:A_HEAD_DIM].astype(BF16)
    iw = iw_ref[:, A_HEAD_DIM:A_HEAD_DIM + IDX_HEADS]
    iq = iq_ref[...]

    score = jnp.zeros((tq, s_len), F32)
    for h in range(IDX_HEADS):
        lg = _dot_nt(iq[:, h * IDX_DIM:(h + 1) * IDX_DIM].astype(BF16), ik)
        score = score + jnp.maximum(lg, 0.0) * iw[:, h:h + 1]
    w_scale = (IDX_HEADS ** -0.5) * (IDX_DIM ** -0.5)
    score = score * w_scale + 0.0

    key_pos = lax.broadcasted_iota(I32, (tq, s_len), 1)
    q_pos = i * tq + lax.broadcasted_iota(I32, (tq, s_len), 0)
    causal = key_pos <= q_pos

    bits = pltpu.bitcast(score, I32)
    key = jnp.where(bits < 0, bits ^ 0x7FFFFFFF, bits)
    key = jnp.where(causal, key, INT_MIN)

    def count_ge(cand):
        return jnp.sum(jnp.where(key >= cand, 1.0, 0.0), axis=-1, keepdims=True)

    lo = jnp.where(count_ge(jnp.zeros((tq, 1), I32)) >= n_sel, 0, INT_MIN).astype(I32)

    def bisect(t, lo):
        cand = lo + jnp.left_shift(jnp.int32(1), 30 - t)
        return jnp.where(count_ge(cand) >= n_sel, cand, lo)

    thr = lax.fori_loop(0, 31, bisect, lo)

    gt = key > thr
    eq = key == thr
    need = n_sel - jnp.sum(jnp.where(gt, 1.0, 0.0), axis=-1, keepdims=True)
    r = lax.broadcasted_iota(I32, (LANES, LANES), 0)
    c = lax.broadcasted_iota(I32, (LANES, LANES), 1)
    tri = jnp.where(r < c, 1.0, 0.0).astype(BF16)
    eq_bf = jnp.where(eq, 1.0, 0.0).astype(BF16)
    carry = jnp.zeros((tq, 1), F32)
    pref = []
    for cb in range(s_len // LANES):
        blk = eq_bf[:, cb * LANES:(cb + 1) * LANES]
        pref.append(_dot(blk, tri) + carry)
        carry = carry + jnp.sum(blk.astype(F32), axis=-1, keepdims=True)
    prefix = jnp.concatenate(pref, axis=1)
    sel = (gt | (eq & (prefix < need))) & causal

    a_scale = A_HEAD_DIM ** -0.5
    qa = qa_ref[...]
    outs = []
    for h in range(A_HEADS):
        sc = _dot_nt(qa[:, h * A_HEAD_DIM:(h + 1) * A_HEAD_DIM].astype(BF16), ka) * a_scale
        outs.append(_softmax_pv(sc, sel, va))
    o_ref[...] = jnp.concatenate(outs, axis=1)


def _causal_segments(s_len, unit):
    ext = s_len
    while s_len // ext < ATTN_MAX_SPLIT and (ext // 2) % unit == 0:
        ext //= 2
    segs = [(0, ext, ext)]
    while ext < s_len:
        segs.append((ext, ext, 2 * ext))
        ext *= 2
    return segs


def _join_segments(outs, batch):
    return jnp.concatenate([o.reshape(batch, -1, o.shape[-1]) for o in outs], axis=1).reshape(-1, outs[0].shape[-1])


def _dsa_attention(iq, qa, kaik, vaiw, batch, s_len, tq=128):
    nq = s_len // tq
    n_sel = min(DSA_TOPK_MAX, s_len // 4)
    hd = A_HEADS * A_HEAD_DIM
    outs = []
    for q0, q_len, kext in _causal_segments(s_len, tq):
        q_off, nqs, kb = q0 // tq, q_len // tq, s_len // kext
        kern = functools.partial(_dsa_kernel, n_sel=n_sel, tq=tq, s_len=kext, q_off=q_off)
        q_map = lambda b, i, q_off=q_off: (b * nq + q_off + i, 0)
        k_map = lambda b, i, kb=kb: (b * kb, 0)
        outs.append(pl.pallas_call(
            kern, grid=(batch, nqs),
            in_specs=[pl.BlockSpec((tq, IDX_HEADS * IDX_DIM), q_map),
                      pl.BlockSpec((tq, hd), q_map),
                      pl.BlockSpec((kext, LANES), k_map),
                      pl.BlockSpec((kext, LANES), k_map),
                      pl.BlockSpec((tq, LANES), q_map)],
            out_specs=pl.BlockSpec((tq, hd), lambda b, i, nqs=nqs: (b * nqs + i, 0)),
            out_shape=jax.ShapeDtypeStruct((batch * q_len, hd), F32),
            compiler_params=_params("parallel", "arbitrary"), name="dsa_attention",
        )(iq, qa, kaik, vaiw, vaiw))
    return _join_segments(outs, batch)


def _moba_kernel(q_ref, k_ref, v_ref, o_ref, *, n_top, s_len, q_off):
    i = pl.program_id(2) + q_off
    tq = MOBA_BLOCK
    nb = s_len // MOBA_BLOCK
    d = B_HEAD_DIM
    scale = d ** -0.5
    key_pos = lax.broadcasted_iota(I32, (tq, s_len), 1)
    q_pos = i * tq + lax.broadcasted_iota(I32, (tq, s_len), 0)
    own_mask = (key_pos >= i * MOBA_BLOCK) & (key_pos <= q_pos)
    blk_lane = lax.broadcasted_iota(I32, (tq, nb), 1)
    expand = jnp.where(lax.broadcasted_iota(I32, (nb, s_len), 1) // MOBA_BLOCK
                       == lax.broadcasted_iota(I32, (nb, s_len), 0), 1.0, 0.0).astype(BF16)
    outs = []
    for hh in range(LANES // d):
        q = q_ref[:, hh * d:(hh + 1) * d].astype(BF16)
        k = k_ref[:, hh * d:(hh + 1) * d]
        v = v_ref[:, hh * d:(hh + 1) * d].astype(BF16)
        k_mean = jnp.sum(k.reshape(nb, MOBA_BLOCK, d), axis=1) / MOBA_BLOCK
        gate = _dot_nt(q, k_mean.astype(BF16))
        gate = jnp.where(blk_lane < i, gate, -jnp.inf)
        rank = jnp.zeros((tq, nb), F32)
        for b2 in range(nb):
            gb = gate[:, b2:b2 + 1]
            beats = (gb > gate) | ((gb == gate) & (b2 < blk_lane))
            rank = rank + jnp.where(beats, 1.0, 0.0)
        sel = jnp.where((rank < n_top) & (blk_lane < i), 1.0, 0.0).astype(BF16)
        selk = _dot(sel, expand)
        mask = (selk > 0.5) | own_mask
        s = _dot_nt(q, k.astype(BF16)) * scale
        outs.append(_softmax_pv(s, mask, v))
    o_ref[...] = jnp.concatenate(outs, axis=1)


def _moba_attention(qb, kb, vb, batch, s_len):
    nb = s_len // MOBA_BLOCK
    n_top = min(MOBA_TOPK, nb - 1)
    hp = B_HEADS * B_HEAD_DIM // LANES
    outs = []
    for q0, q_len, kext in _causal_segments(s_len, MOBA_BLOCK):
        q_off, nqs, kbl = q0 // MOBA_BLOCK, q_len // MOBA_BLOCK, s_len // kext
        kern = functools.partial(_moba_kernel, n_top=n_top, s_len=kext, q_off=q_off)
        k_map = lambda b, g, i, kbl=kbl: (b * kbl, g)
        outs.append(pl.pallas_call(
            kern, grid=(batch, hp, nqs),
            in_specs=[pl.BlockSpec((MOBA_BLOCK, LANES), lambda b, g, i, q_off=q_off: (b * nb + q_off + i, g)),
                      pl.BlockSpec((kext, LANES), k_map),
                      pl.BlockSpec((kext, LANES), k_map)],
            out_specs=pl.BlockSpec((MOBA_BLOCK, LANES), lambda b, g, i, nqs=nqs: (b * nqs + i, g)),
            out_shape=jax.ShapeDtypeStruct((batch * q_len, B_HEADS * B_HEAD_DIM), F32),
            compiler_params=_params("parallel", "parallel", "arbitrary"), name="moba_attention",
        )(qb, kb, vb))
    return _join_segments(outs, batch)


MLA_GROUP = 4


def _mla_kernel(qn_ref, qr_ref, kn_ref, kr_ref, v_ref, o_ref, *, tq, s_len, q_off):
    i = pl.program_id(2) + q_off
    scale = (C_NOPE + C_ROPE) ** -0.5
    key_pos = lax.broadcasted_iota(I32, (tq, s_len), 1)
    q_pos = i * tq + lax.broadcasted_iota(I32, (tq, s_len), 0)
    causal = key_pos <= q_pos
    kr = kr_ref[:, 0:C_ROPE].astype(BF16)
    outs = []
    for hh in range(MLA_GROUP):
        qn = qn_ref[:, hh * C_NOPE:(hh + 1) * C_NOPE].astype(BF16)
        qr = qr_ref[:, hh * C_ROPE:(hh + 1) * C_ROPE].astype(BF16)
        kn = kn_ref[:, hh * C_NOPE:(hh + 1) * C_NOPE].astype(BF16)
        v = v_ref[:, hh * C_V:(hh + 1) * C_V].astype(BF16)
        s = (_dot_nt(qn, kn) + _dot_nt(qr, kr)) * scale
        outs.append(_softmax_pv(s, causal, v))
    o_ref[...] = jnp.concatenate(outs, axis=1)


def _mla_attention(qn, qr, kn, kr, v, batch, s_len, tq=256):
    nq = s_len // tq
    ng = C_HEADS // MLA_GROUP
    outs = []
    for q0, q_len, kext in _causal_segments(s_len, tq):
        q_off, nqs, kbl = q0 // tq, q_len // tq, s_len // kext
        kern = functools.partial(_mla_kernel, tq=tq, s_len=kext, q_off=q_off)
        q_map = lambda b, g, i, q_off=q_off: (b * nq + q_off + i, g)
        k_map = lambda b, g, i, kbl=kbl: (b * kbl, g)
        outs.append(pl.pallas_call(
            kern, grid=(batch, ng, nqs),
            in_specs=[pl.BlockSpec((tq, MLA_GROUP * C_NOPE), q_map),
                      pl.BlockSpec((tq, MLA_GROUP * C_ROPE), q_map),
                      pl.BlockSpec((kext, MLA_GROUP * C_NOPE), k_map),
                      pl.BlockSpec((kext, LANES), lambda b, g, i, kbl=kbl: (b * kbl, 0)),
                      pl.BlockSpec((kext, MLA_GROUP * C_V), k_map)],
            out_specs=pl.BlockSpec((tq, MLA_GROUP * C_V), lambda b, g, i, nqs=nqs: (b * nqs + i, g)),
            out_shape=jax.ShapeDtypeStruct((batch * q_len, C_HEADS * C_V), F32),
            compiler_params=_params("parallel", "parallel", "arbitrary"), name="mla_attention",
        )(qn, qr, kn, kr, v))
    return _join_segments(outs, batch)


def _topk_rows(sc, k):
    r = sc.shape[0]
    riota = lax.broadcasted_iota(I32, sc.shape, 0)
    vals, rows = [], []
    for _ in range(k):
        m = jnp.max(sc, axis=0, keepdims=True)
        pos = jnp.min(jnp.where(sc == m, riota, r), axis=0, keepdims=True)
        sc = jnp.where(riota == pos, -jnp.inf, sc)
        vals.append(m)
        rows.append(pos)
    return jnp.concatenate(vals, axis=0), jnp.concatenate(rows, axis=0)


def _peer_candidates():
    k = PEER_TOPK_HALF
    rows = [(0, b) for b in range(k)]
    rows += [(a, b) for a in range(1, 8) for b in range(8)]
    rows += [(a, 0) for a in range(8, k)]
    return np.array([a * k + b if (a + 1) * (b + 1) <= PEER_TOPK else -1 for a, b in rows], np.int32)


def _select_rows(tab, row):
    riota = lax.broadcasted_iota(I32, tab.shape, 0)
    return jnp.sum(jnp.where(riota == row, tab, 0), axis=0, keepdims=True)


def _peer_topk_kernel(q_ref, keys_ref, pos_ref, e_ref, g_ref, *, idx_offset):
    kh = PEER_TOPK_HALF
    pos_tab = pos_ref[...]
    big = kh * kh
    e_rows, g_rows = [], []
    for h in range(PEER_HEADS):
        half_v, half_i = [], []
        for p in range(2):
            c0 = (h * 2 + p) * PEER_NKEYS
            qh = q_ref[:, c0:c0 + PEER_NKEYS].astype(BF16)
            sc = _dot_nt(keys_ref[h, p], qh)
            v, ix = _topk_rows(sc, kh)
            half_v.append(v)
            half_i.append(ix)
        v0, v1 = half_v
        cand = jnp.concatenate(
            [v0[0:1] + v1] + [v0[a:a + 1] + v1[0:8] for a in range(1, 8)] + [v0[8:kh] + v1[0:1]], axis=0)
        cand = jnp.where(pos_tab >= 0, cand, -jnp.inf)
        pos_key = jnp.where(pos_tab >= 0, pos_tab, big)
        tops = []
        for _ in range(PEER_TOPK):
            m = jnp.max(cand, axis=0, keepdims=True)
            pos = jnp.min(jnp.where(cand == m, pos_key, big), axis=0, keepdims=True)
            cand = jnp.where(pos_key == pos, -jnp.inf, cand)
            tops.append(m)
            e_rows.append(_select_rows(half_i[0], pos // kh) * PEER_NKEYS
                          + _select_rows(half_i[1], pos % kh) + idx_offset)
        top_s = jnp.concatenate(tops, axis=0)
        ex = jnp.exp(top_s - top_s[0:1, :])
        g_rows.append(ex / jnp.sum(ex, axis=0, keepdims=True))
    e_ref[...] = jnp.concatenate(e_rows, axis=0).T
    g_ref[...] = jnp.concatenate(g_rows, axis=0).T


def _peer_topk(q, keys_bf, idx_offset, t=128):
    n = q.shape[0]
    pos_tab = jnp.asarray(np.tile(_peer_candidates()[:, None], (1, t)))
    out = pl.BlockSpec((t, PEER_SEL), lambda i: (i, 0))
    return pl.pallas_call(
        functools.partial(_peer_topk_kernel, idx_offset=idx_offset), grid=(n // t,),
        in_specs=[pl.BlockSpec((t, PEER_HEADS * PEER_QDIM), lambda i: (i, 0)),
                  pl.BlockSpec(keys_bf.shape, lambda i: (0, 0, 0, 0)),
                  pl.BlockSpec(pos_tab.shape, lambda i: (0, 0))],
        out_specs=[out, out],
        out_shape=[jax.ShapeDtypeStruct((n, PEER_SEL), I32), jax.ShapeDtypeStruct((n, PEER_SEL), F32)],
        compiler_params=_params("parallel"), name="peer_topk",
    )(q, keys_bf, pos_tab)


PEER_SEL = PEER_HEADS * PEER_TOPK

SC_CORES = 2
SC_SUBCORES = 16
SC_LANES = 16
SC_WORKERS = SC_CORES * SC_SUBCORES
SC_ROWS = 32
SC_TOK = 8
SC_CHUNKS = SC_TOK * PEER_SEL // SC_ROWS
SC_VCOLS = 256
SC_UNROLL = 1


def _sc_mesh():
    return plsc.VectorSubcoreMesh(core_axis_name="c", subcore_axis_name="s",
                                  num_cores=SC_CORES, num_subcores=SC_SUBCORES)


def _pack_table(t):
    half = t.shape[1] // 2
    b = lax.bitcast_convert_type(t.astype(BF16), jnp.uint16).astype(jnp.uint32)
    return lax.bitcast_convert_type(b[:, :half] | (b[:, half:] << 16), I32)


def _unpack_pair(words):
    lo = lax.bitcast_convert_type(jnp.left_shift(words, 16), F32)
    hi = lax.bitcast_convert_type(jnp.bitwise_and(words, jnp.int32(-65536)), F32)
    return lo, hi


def _sc_chunk_pipeline(gather, compute):
    gather(0, 0).start()

    @pl.loop(0, SC_CHUNKS, step=2)
    def _(q):
        gather(q + 1, 1).start()
        gather(q, 0).wait()
        compute(q, 0)

        @pl.when(q + 2 < SC_CHUNKS)
        def _():
            gather(q + 2, 0).start()

        gather(q + 1, 1).wait()
        compute(q + 1, 1)


def _sc_u_body(idx_hbm, h_hbm, u_hbm, act_hbm, idx_v, h_v, rows_v, act_v, sem0, sem1, *, tpw, d):
    tok_base = (lax.axis_index("s") * SC_CORES + lax.axis_index("c")) * tpw
    sems = (sem0, sem1)

    def gather(q, b):
        return pltpu.make_async_copy(u_hbm.at[idx_v.at[pl.ds(q * SC_ROWS, SC_ROWS)]], rows_v.at[b], sems[b])

    def compute(q, b):
        tok = q // (PEER_SEL // SC_ROWS)
        lane = lax.iota(I32, SC_LANES)
        for g in range(SC_ROWS // SC_LANES):
            def body(c, accs, g=g):
                x_lo = h_v[tok, pl.ds(c * SC_LANES, SC_LANES)]
                x_hi = h_v[tok, pl.ds(d // 2 + c * SC_LANES, SC_LANES)]
                out = []
                for r, a in enumerate(accs):
                    lo, hi = _unpack_pair(rows_v[b, g * SC_LANES + r, pl.ds(c * SC_LANES, SC_LANES)])
                    out.append(a + lo * x_lo + hi * x_hi)
                return tuple(out)
            accs = lax.fori_loop(0, d // (2 * SC_LANES), body,
                                 tuple(jnp.zeros((SC_LANES,), F32) for _ in range(SC_LANES)),
                                 unroll=SC_UNROLL)
            out = jnp.zeros((SC_LANES,), F32)
            for r in range(SC_LANES):
                out = jnp.where(lane == r, jnp.sum(accs[r]), out)
            act_v[pl.ds(q * SC_ROWS + g * SC_LANES, SC_LANES)] = out

    @pl.loop(0, tpw // SC_TOK)
    def _(blk):
        tok0 = tok_base + blk * SC_TOK
        pltpu.sync_copy(idx_hbm.at[pl.ds(tok0 * PEER_SEL, SC_TOK * PEER_SEL)], idx_v)
        pltpu.sync_copy(h_hbm.at[pl.ds(tok0, SC_TOK)], h_v)
        _sc_chunk_pipeline(gather, compute)
        pltpu.sync_copy(act_v, act_hbm.at[pl.ds(tok0 * PEER_SEL, SC_TOK * PEER_SEL)])


def _sc_peer_u(idx_flat, h, u_tab):
    n, d = h.shape
    assert u_tab.shape[1] * 2 == d and u_tab.dtype == I32
    assert n % (SC_WORKERS * SC_TOK) == 0 and d % SC_VCOLS == 0
    body = functools.partial(_sc_u_body, tpw=n // SC_WORKERS, d=d)
    return pl.kernel(
        body,
        out_type=jax.ShapeDtypeStruct((n * PEER_SEL,), F32),
        mesh=_sc_mesh(),
        scratch_types=[pltpu.VMEM((SC_TOK * PEER_SEL,), I32),
                       pltpu.VMEM((SC_TOK, d), F32),
                       pltpu.VMEM((2, SC_ROWS, d // 2), I32),
                       pltpu.VMEM((SC_TOK * PEER_SEL,), F32),
                       pltpu.SemaphoreType.DMA, pltpu.SemaphoreType.DMA],
        compiler_params=pltpu.CompilerParams(needs_layout_passes=False),
        name="sc_peer_u",
    )(idx_flat, h, u_tab)


def _sc_v_body(idx_hbm, w_hbm, x_hbm, v_hbm, o_hbm, idx_v, w_v, out_v, rows_v, sem0, sem1, *, tpw, d):
    tok_base = (lax.axis_index("s") * SC_CORES + lax.axis_index("c")) * tpw
    sems = (sem0, sem1)
    half = d // 2
    nword = SC_VCOLS // (2 * SC_LANES)

    def gather(q, b):
        return pltpu.make_async_copy(v_hbm.at[idx_v.at[pl.ds(q * SC_ROWS, SC_ROWS)]], rows_v.at[b], sems[b])

    def compute(q, b):
        tok = q // (PEER_SEL // SC_ROWS)
        for cp in range(d // SC_VCOLS):
            w0 = cp * nword * SC_LANES
            cols = [half * part + w0 + a * SC_LANES for a in range(nword) for part in range(2)]

            def body(r, accs, w0=w0):
                wv = plsc.load_gather(w_v, [jnp.full((SC_LANES,), q * SC_ROWS + r, I32)])
                out = []
                for a in range(nword):
                    lo, hi = _unpack_pair(rows_v[b, r, pl.ds(w0 + a * SC_LANES, SC_LANES)])
                    out += [accs[2 * a] + wv * lo, accs[2 * a + 1] + wv * hi]
                return tuple(out)

            accs = lax.fori_loop(0, SC_ROWS, body,
                                 tuple(out_v[tok, pl.ds(c, SC_LANES)] for c in cols))
            for c, acc in zip(cols, accs):
                out_v[tok, pl.ds(c, SC_LANES)] = acc

    @pl.loop(0, tpw // SC_TOK)
    def _(blk):
        tok0 = tok_base + blk * SC_TOK
        pltpu.sync_copy(idx_hbm.at[pl.ds(tok0 * PEER_SEL, SC_TOK * PEER_SEL)], idx_v)
        pltpu.sync_copy(w_hbm.at[pl.ds(tok0 * PEER_SEL, SC_TOK * PEER_SEL)], w_v)
        pltpu.sync_copy(x_hbm.at[pl.ds(tok0, SC_TOK)], out_v)
        _sc_chunk_pipeline(gather, compute)
        pltpu.sync_copy(out_v, o_hbm.at[pl.ds(tok0, SC_TOK)])


def _sc_peer_v(idx_flat, w_flat, x, v_tab):
    n, d = x.shape
    assert v_tab.shape[1] * 2 == d and v_tab.dtype == I32
    assert n % (SC_WORKERS * SC_TOK) == 0 and d % SC_VCOLS == 0
    body = functools.partial(_sc_v_body, tpw=n // SC_WORKERS, d=d)
    return pl.kernel(
        body,
        out_type=jax.ShapeDtypeStruct((n, d), F32),
        mesh=_sc_mesh(),
        scratch_types=[pltpu.VMEM((SC_TOK * PEER_SEL,), I32),
                       pltpu.VMEM((SC_TOK * PEER_SEL,), F32),
                       pltpu.VMEM((SC_TOK, d), F32),
                       pltpu.VMEM((2, SC_ROWS, d // 2), I32),
                       pltpu.SemaphoreType.DMA, pltpu.SemaphoreType.DMA],
        compiler_params=pltpu.CompilerParams(needs_layout_passes=False),
        name="sc_peer_v",
    )(idx_flat, w_flat, x, v_tab)


def _peer_gate_kernel(g_ref, act_ref, w_ref):
    act = act_ref[...]
    w_ref[...] = g_ref[...] * (0.5 * act * (1.0 + lax.erf(act * (2.0 ** -0.5))))


def _peer_gate(g, act, tm=2048):
    n = g.shape[0]
    spec = pl.BlockSpec((tm, PEER_SEL), lambda i: (i, 0))
    return pl.pallas_call(
        _peer_gate_kernel, grid=(n // tm,), in_specs=[spec, spec], out_specs=spec,
        out_shape=jax.ShapeDtypeStruct((n, PEER_SEL), F32),
        compiler_params=_params("parallel"), name="peer_gate",
    )(g, act)


def _pad_cols(w, width):
    return jnp.pad(w, ((0, 0), (0, width - w.shape[1])))


def _even_weights(w_in):
    hd = A_HEADS * A_HEAD_DIM
    cols = np.cumsum([0, hd, A_HEAD_DIM, A_HEAD_DIM, IDX_HEADS * IDX_DIM, IDX_DIM, IDX_HEADS,
                      B_HEADS * B_HEAD_DIM, B_HEADS * B_HEAD_DIM, B_HEADS * B_HEAD_DIM])
    qa, ka, va, iq, ik, iw, qb, kb, vb = [w_in[:, cols[j]:cols[j + 1]] for j in range(9)]
    vaiw = _pad_cols(jnp.concatenate([va, iw], axis=1), LANES)
    w = jnp.concatenate([qa, iq, qb, kb, vb, ka, ik, vaiw], axis=1).astype(BF16)
    groups = [(0, 512, "r64"), (512, 512, "r64"), (1024, 512, "r64"), (1536, 512, "r64"),
              (2048, 512, None), (2560, LANES, "r64"), (2560 + LANES, LANES, None)]
    return w, groups


def _even_mixer(x, gain, w_in, w_out, tables, batch, s_len):
    w, groups = _even_weights(w_in)
    qa, iq, qb, kb, vb, kaik, vaiw = _fused_proj(x, gain, w, groups, tables)
    out_a = _dsa_attention(iq, qa, kaik, vaiw, batch, s_len)
    out_b = _moba_attention(qb, kb, vb, batch, s_len)
    hd = A_HEADS * A_HEAD_DIM
    w_out = w_out.astype(BF16)
    return _outproj_residual(x, [out_a, out_b], [w_out[:hd], w_out[hd:]])


def _mla_mixer(x, gain, w_in, q_norm, kv_norm, w_uq, w_ukv, w_out, tables, batch, s_len):
    w1 = _pad_cols(w_in, C_Q_RANK + C_KV_RANK + LANES).astype(BF16)
    groups1 = [(0, C_Q_RANK, None), (C_Q_RANK, C_KV_RANK, None), (C_Q_RANK + C_KV_RANK, LANES, "r32")]
    cq, ckv, kr = _fused_proj(x, gain, w1, groups1, tables)
    qd = C_NOPE + C_ROPE
    nope_cols = np.concatenate([np.arange(h * qd, h * qd + C_NOPE) for h in range(C_HEADS)])
    rope_cols = np.concatenate([np.arange(h * qd + C_NOPE, (h + 1) * qd) for h in range(C_HEADS)])
    w2 = jnp.concatenate([w_uq[:, nope_cols], w_uq[:, rope_cols]], axis=1).astype(BF16)
    groups2 = [(0, C_HEADS * C_NOPE, None), (C_HEADS * C_NOPE, C_HEADS * C_ROPE, "r32")]
    qn, qr = _fused_proj(cq, q_norm, w2, groups2, tables)
    kd = C_NOPE + C_V
    kn_cols = np.concatenate([np.arange(h * kd, h * kd + C_NOPE) for h in range(C_HEADS)])
    v_cols = np.concatenate([np.arange(h * kd + C_NOPE, (h + 1) * kd) for h in range(C_HEADS)])
    w3 = jnp.concatenate([w_ukv[:, kn_cols], w_ukv[:, v_cols]], axis=1).astype(BF16)
    groups3 = [(0, C_HEADS * C_NOPE, None), (C_HEADS * C_NOPE, C_HEADS * C_V, None)]
    kn, v = _fused_proj(ckv, kv_norm, w3, groups3, tables)
    o = _mla_attention(qn, qr, kn, kr, v, batch, s_len)
    return _outproj_residual(x, [o], [w_out.astype(BF16)])


def _peer_ffn(x, gain, w_q, sub_keys, u_all, v_all, layer):
    n, d = x.shape
    n_experts = PEER_NKEYS * PEER_NKEYS
    q, h = _fused_proj(x, gain, w_q.astype(BF16), [(0, PEER_HEADS * PEER_QDIM, None)], {}, emit_h=True)
    idx, g = _peer_topk(q, sub_keys.astype(BF16), layer * n_experts)
    idx_flat = idx.reshape(n * PEER_SEL)
    act = _sc_peer_u(idx_flat, h, u_all)
    w = _peer_gate(g, act.reshape(n, PEER_SEL))
    return _sc_peer_v(idx_flat, w.reshape(n * PEER_SEL), x, v_all)


def kernel(x, positions, attn_norm, ffn_norm, final_norm, hyb_w_in, hyb_w_out, mla_w_in, mla_q_norm,
           mla_kv_norm, mla_w_uq, mla_w_ukv, mla_w_out, peer_w_q, peer_sub_keys, peer_u, peer_v):
    batch, s_len, d = x.shape
    depth = attn_norm.shape[0]
    u_all = _pack_table(peer_u.reshape(-1, d))
    v_all = _pack_table(peer_v.reshape(-1, d))
    gb = batch // BATCH_GROUPS
    gn = gb * s_len
    xs, tabs = [], []
    for g in range(BATCH_GROUPS):
        pos = positions[g * gb:(g + 1) * gb].reshape(gn, 1).astype(I32)
        tabs.append({"r64": _rope_tables(pos, A_HEAD_DIM), "r32": _rope_tables(pos, C_ROPE)})
        xs.append(x[g * gb:(g + 1) * gb].reshape(gn, d))
    for i in range(depth):
        j = i // 2
        for g in range(BATCH_GROUPS):
            if i % 2 == 0:
                xs[g] = _even_mixer(xs[g], attn_norm[i], hyb_w_in[j], hyb_w_out[j], tabs[g], gb, s_len)
            else:
                xs[g] = _mla_mixer(xs[g], attn_norm[i], mla_w_in[j], mla_q_norm[j], mla_kv_norm[j],
                                   mla_w_uq[j], mla_w_ukv[j], mla_w_out[j], tabs[g], gb, s_len)
            xs[g] = _peer_ffn(xs[g], ffn_norm[i], peer_w_q[i], peer_sub_keys[i], u_all, v_all, i)
    outs = [_rmsnorm(xg, final_norm).reshape(gb, s_len, d) for xg in xs]
    return jnp.concatenate(outs, axis=0)
```

```python
import functools

import numpy as np
import jax
import jax.numpy as jnp
from jax import lax
from jax.experimental import pallas as pl
from jax.experimental.pallas import tpu as pltpu
from jax.experimental.pallas import tpu_sc as plsc

NORM_EPS = 1e-6
ROPE_THETA = 10000.0

A_HEADS = 8
A_HEAD_DIM = 64
IDX_HEADS = 8
IDX_DIM = 64
DSA_TOPK_MAX = 256

B_HEADS = 8
B_HEAD_DIM = 64
MOBA_BLOCK = 256
MOBA_TOPK = 3

C_HEADS = 16
C_NOPE = 64
C_ROPE = 32
C_V = 64
C_Q_RANK = 768
C_KV_RANK = 256

PEER_HEADS = 8
PEER_NKEYS = 128
PEER_QDIM = 256
PEER_TOPK_HALF = 16
PEER_TOPK = 16

BATCH_GROUPS = 8
ATTN_MAX_SPLIT = 4
LANES = 128
VMEM_LIMIT = 56 * 1024 * 1024

F32 = jnp.float32
BF16 = jnp.bfloat16
I32 = jnp.int32
INT_MIN = -(2 ** 31)


def _params(*sem):
    return pltpu.CompilerParams(dimension_semantics=sem, vmem_limit_bytes=VMEM_LIMIT)


def _dot(a, b):
    return jnp.dot(a, b, preferred_element_type=F32)


def _dot_nt(a, b):
    return lax.dot_general(a, b, (((1,), (1,)), ((), ())), preferred_element_type=F32)


def _rope_table_kernel(pos_ref, inv_ref, sign_ref, cos_ref, sin_ref):
    ang = pos_ref[...].astype(F32) * inv_ref[...]
    cos_ref[...] = jnp.cos(ang)
    sin_ref[...] = jnp.sin(ang) * sign_ref[...]


def _rope_tables(pos, dim, tm=1024):
    n = pos.shape[0]
    half = dim // 2
    inv = 1.0 / (ROPE_THETA ** (jnp.arange(0, dim, 2, dtype=F32) / dim))
    reps = LANES // dim
    inv_l = jnp.tile(jnp.concatenate([inv, inv]), reps)[None, :]
    sign_l = jnp.tile(jnp.concatenate([-jnp.ones(half, F32), jnp.ones(half, F32)]), reps)[None, :]
    out = jax.ShapeDtypeStruct((n, LANES), F32)
    return pl.pallas_call(
        _rope_table_kernel,
        grid=(n // tm,),
        in_specs=[pl.BlockSpec((tm, 1), lambda i: (i, 0)),
                  pl.BlockSpec((1, LANES), lambda i: (0, 0)),
                  pl.BlockSpec((1, LANES), lambda i: (0, 0))],
        out_specs=[pl.BlockSpec((tm, LANES), lambda i: (i, 0))] * 2,
        out_shape=[out, out],
        compiler_params=_params("parallel"),
        name="rope_tables",
    )(pos, inv_l, sign_l)


def _rope_epilogue(acc, cos, sin, half):
    wd = acc.shape[1]
    reps = wd // LANES
    if reps > 1:
        cos = jnp.concatenate([cos] * reps, axis=1)
        sin = jnp.concatenate([sin] * reps, axis=1)
    lane = lax.broadcasted_iota(I32, acc.shape, 1)
    first = (lane % (2 * half)) < half
    partner = jnp.where(first, pltpu.roll(acc, wd - half, 1), pltpu.roll(acc, half, 1))
    return acc * cos + partner * sin


def _proj_kernel(*refs, groups, has_gain, emit_h, tab_names):
    it = iter(refs)
    x_ref = next(it)
    g_ref = next(it) if has_gain else None
    w_ref = next(it)
    tabs = {name: (next(it), next(it)) for name in tab_names}
    out_refs = [next(it) for _ in groups]
    h_ref = next(it) if emit_h else None

    x = x_ref[...]
    if has_gain:
        ms = jnp.mean(x * x, axis=-1, keepdims=True)
        h = x * lax.rsqrt(ms + NORM_EPS) * g_ref[...]
    else:
        h = x
    if emit_h:
        h_ref[...] = h
    hb = h.astype(BF16)
    for o_ref, (c0, wd, rope) in zip(out_refs, groups):
        acc = _dot(hb, w_ref[:, c0:c0 + wd])
        if rope is not None:
            cos_ref, sin_ref = tabs[rope]
            acc = _rope_epilogue(acc, cos_ref[...], sin_ref[...], 32 if rope == "r64" else 16)
        o_ref[...] = acc


def _fused_proj(x, gain, w, groups, tables, emit_h=False, tm=256):
    n, k = x.shape
    tab_names = sorted({g[2] for g in groups if g[2] is not None})
    args = [x]
    in_specs = [pl.BlockSpec((tm, k), lambda i: (i, 0))]
    if gain is not None:
        args.append(gain.reshape(1, k).astype(F32))
        in_specs.append(pl.BlockSpec((1, k), lambda i: (0, 0)))
    args.append(w)
    in_specs.append(pl.BlockSpec(w.shape, lambda i: (0, 0)))
    for name in tab_names:
        for t in tables[name]:
            args.append(t)
            in_specs.append(pl.BlockSpec((tm, LANES), lambda i: (i, 0)))
    out_shape = [jax.ShapeDtypeStruct((n, wd), F32) for (_, wd, _) in groups]
    out_specs = [pl.BlockSpec((tm, wd), lambda i: (i, 0)) for (_, wd, _) in groups]
    if emit_h:
        out_shape.append(jax.ShapeDtypeStruct((n, k), F32))
        out_specs.append(pl.BlockSpec((tm, k), lambda i: (i, 0)))
    kern = functools.partial(_proj_kernel, groups=tuple(groups), has_gain=gain is not None,
                             emit_h=emit_h, tab_names=tuple(tab_names))
    return pl.pallas_call(
        kern, grid=(n // tm,), in_specs=in_specs, out_specs=out_specs, out_shape=out_shape,
        compiler_params=_params("parallel"), name="fused_proj",
    )(*args)


def _outproj_kernel(*refs, n_in):
    res_ref = refs[0]
    a_refs = refs[1:1 + n_in]
    w_refs = refs[1 + n_in:1 + 2 * n_in]
    o_ref = refs[-1]
    acc = res_ref[...]
    for a_ref, w_ref in zip(a_refs, w_refs):
        acc = acc + _dot(a_ref[...].astype(BF16), w_ref[...])
    o_ref[...] = acc


def _outproj_residual(res, acts, ws, tm=256):
    n, d = res.shape
    in_specs = [pl.BlockSpec((tm, d), lambda i: (i, 0))]
    in_specs += [pl.BlockSpec((tm, a.shape[1]), lambda i: (i, 0)) for a in acts]
    in_specs += [pl.BlockSpec(w.shape, lambda i: (0, 0)) for w in ws]
    return pl.pallas_call(
        functools.partial(_outproj_kernel, n_in=len(acts)),
        grid=(n // tm,), in_specs=in_specs,
        out_specs=pl.BlockSpec((tm, d), lambda i: (i, 0)),
        out_shape=jax.ShapeDtypeStruct((n, d), F32),
        compiler_params=_params("parallel"), name="outproj_residual",
    )(res, *acts, *ws)


def _rmsnorm_kernel(x_ref, g_ref, o_ref):
    x = x_ref[...]
    ms = jnp.mean(x * x, axis=-1, keepdims=True)
    o_ref[...] = x * lax.rsqrt(ms + NORM_EPS) * g_ref[...]


def _rmsnorm(x, gain, tm=512):
    n, d = x.shape
    return pl.pallas_call(
        _rmsnorm_kernel, grid=(n // tm,),
        in_specs=[pl.BlockSpec((tm, d), lambda i: (i, 0)), pl.BlockSpec((1, d), lambda i: (0, 0))],
        out_specs=pl.BlockSpec((tm, d), lambda i: (i, 0)),
        out_shape=jax.ShapeDtypeStruct((n, d), F32),
        compiler_params=_params("parallel"), name="final_rmsnorm",
    )(x, gain.reshape(1, d))


def _softmax_pv(s, mask, v_bf):
    s = jnp.where(mask, s, -jnp.inf)
    m = jnp.max(s, axis=-1, keepdims=True)
    p = jnp.exp(s - m)
    l = jnp.sum(p, axis=-1, keepdims=True)
    return _dot(p.astype(BF16), v_bf) / l


def _dsa_kernel(iq_ref, qa_ref, kaik_ref, va_ref, iw_ref, o_ref, *, n_sel, tq, s_len, q_off):
    i = pl.program_id(1) + q_off
    kaik = kaik_ref[...]
    ka = kaik[:, 0:A_HEAD_DIM].astype(BF16)
    ik = kaik[:, A_HEAD_DIM:A_HEAD_DIM + IDX_DIM].astype(BF16)
    va = va_ref[:, 0:A_HEAD_DIM].astype(BF16)
    iw = iw_ref[:, A_HEAD_DIM:A_HEAD_DIM + IDX_HEADS]
    iq = iq_ref[...]

    score = jnp.zeros((tq, s_len), F32)
    for h in range(IDX_HEADS):
        lg = _dot_nt(iq[:, h * IDX_DIM:(h + 1) * IDX_DIM].astype(BF16), ik)
        score = score + jnp.maximum(lg, 0.0) * iw[:, h:h + 1]
    w_scale = (IDX_HEADS ** -0.5) * (IDX_DIM ** -0.5)
    score = score * w_scale + 0.0

    key_pos = lax.broadcasted_iota(I32, (tq, s_len), 1)
    q_pos = i * tq + lax.broadcasted_iota(I32, (tq, s_len), 0)
    causal = key_pos <= q_pos

    bits = pltpu.bitcast(score, I32)
    key = jnp.where(bits < 0, bits ^ 0x7FFFFFFF, bits)
    key = jnp.where(causal, key, INT_MIN)

    def count_ge(cand):
        return jnp.sum(jnp.where(key >= cand, 1.0, 0.0), axis=-1, keepdims=True)

    lo = jnp.where(count_ge(jnp.zeros((tq, 1), I32)) >= n_sel, 0, INT_MIN).astype(I32)

    def bisect(t, lo):
        cand = lo + jnp.left_shift(jnp.int32(1), 30 - t)
        return jnp.where(count_ge(cand) >= n_sel, cand, lo)

    thr = lax.fori_loop(0, 31, bisect, lo)

    gt = key > thr
    eq = key == thr
    need = n_sel - jnp.sum(jnp.where(gt, 1.0, 0.0), axis=-1, keepdims=True)
    r = lax.broadcasted_iota(I32, (LANES, LANES), 0)
    c = lax.broadcasted_iota(I32, (LANES, LANES), 1)
    tri = jnp.where(r < c, 1.0, 0.0).astype(BF16)
    eq_bf = jnp.where(eq, 1.0, 0.0).astype(BF16)
    carry = jnp.zeros((tq, 1), F32)
    pref = []
    for cb in range(s_len // LANES):
        blk = eq_bf[:, cb * LANES:(cb + 1) * LANES]
        pref.append(_dot(blk, tri) + carry)
        carry = carry + jnp.sum(blk.astype(F32), axis=-1, keepdims=True)
    prefix = jnp.concatenate(pref, axis=1)
    sel = (gt | (eq & (prefix < need))) & causal

    a_scale = A_HEAD_DIM ** -0.5
    qa = qa_ref[...]
    outs = []
    for h in range(A_HEADS):
        sc = _dot_nt(qa[:, h * A_HEAD_DIM:(h + 1) * A_HEAD_DIM].astype(BF16), ka) * a_scale
        outs.append(_softmax_pv(sc, sel, va))
    o_ref[...] = jnp.concatenate(outs, axis=1)


def _causal_segments(s_len, unit):
    ext = s_len
    while s_len // ext < ATTN_MAX_SPLIT and (ext // 2) % unit == 0:
        ext //= 2
    segs = [(0, ext, ext)]
    while ext < s_len:
        segs.append((ext, ext, 2 * ext))
        ext *= 2
    return segs


def _join_segments(outs, batch):
    return jnp.concatenate([o.reshape(batch, -1, o.shape[-1]) for o in outs], axis=1).reshape(-1, outs[0].shape[-1])


def _dsa_attention(iq, qa, kaik, vaiw, batch, s_len, tq=128):
    nq = s_len // tq
    n_sel = min(DSA_TOPK_MAX, s_len // 4)
    hd = A_HEADS * A_HEAD_DIM
    outs = []
    for q0, q_len, kext in _causal_segments(s_len, tq):
        q_off, nqs, kb = q0 // tq, q_len // tq, s_len // kext
        kern = functools.partial(_dsa_kernel, n_sel=n_sel, tq=tq, s_len=kext, q_off=q_off)
        q_map = lambda b, i, q_off=q_off: (b * nq + q_off + i, 0)
        k_map = lambda b, i, kb=kb: (b * kb, 0)
        outs.append(pl.pallas_call(
            kern, grid=(batch, nqs),
            in_specs=[pl.BlockSpec((tq, IDX_HEADS * IDX_DIM), q_map),
                      pl.BlockSpec((tq, hd), q_map),
                      pl.BlockSpec((kext, LANES), k_map),
                      pl.BlockSpec((kext, LANES), k_map),
                      pl.BlockSpec((tq, LANES), q_map)],
            out_specs=pl.BlockSpec((tq, hd), lambda b, i, nqs=nqs: (b * nqs + i, 0)),
            out_shape=jax.ShapeDtypeStruct((batch * q_len, hd), F32),
            compiler_params=_params("parallel", "arbitrary"), name="dsa_attention",
        )(iq, qa, kaik, vaiw, vaiw))
    return _join_segments(outs, batch)


def _moba_kernel(q_ref, k_ref, v_ref, o_ref, *, n_top, s_len, q_off):
    i = pl.program_id(2) + q_off
    tq = MOBA_BLOCK
    nb = s_len // MOBA_BLOCK
    d = B_HEAD_DIM
    scale = d ** -0.5
    key_pos = lax.broadcasted_iota(I32, (tq, s_len), 1)
    q_pos = i * tq + lax.broadcasted_iota(I32, (tq, s_len), 0)
    own_mask = (key_pos >= i * MOBA_BLOCK) & (key_pos <= q_pos)
    blk_lane = lax.broadcasted_iota(I32, (tq, nb), 1)
    expand = jnp.where(lax.broadcasted_iota(I32, (nb, s_len), 1) // MOBA_BLOCK
                       == lax.broadcasted_iota(I32, (nb, s_len), 0), 1.0, 0.0).astype(BF16)
    outs = []
    for hh in range(LANES // d):
        q = q_ref[:, hh * d:(hh + 1) * d].astype(BF16)
        k = k_ref[:, hh * d:(hh + 1) * d]
        v = v_ref[:, hh * d:(hh + 1) * d].astype(BF16)
        k_mean = jnp.sum(k.reshape(nb, MOBA_BLOCK, d), axis=1) / MOBA_BLOCK
        gate = _dot_nt(q, k_mean.astype(BF16))
        gate = jnp.where(blk_lane < i, gate, -jnp.inf)
        rank = jnp.zeros((tq, nb), F32)
        for b2 in range(nb):
            gb = gate[:, b2:b2 + 1]
            beats = (gb > gate) | ((gb == gate) & (b2 < blk_lane))
            rank = rank + jnp.where(beats, 1.0, 0.0)
        sel = jnp.where((rank < n_top) & (blk_lane < i), 1.0, 0.0).astype(BF16)
        selk = _dot(sel, expand)
        mask = (selk > 0.5) | own_mask
        s = _dot_nt(q, k.astype(BF16)) * scale
        outs.append(_softmax_pv(s, mask, v))
    o_ref[...] = jnp.concatenate(outs, axis=1)


def _moba_attention(qb, kb, vb, batch, s_len):
    nb = s_len // MOBA_BLOCK
    n_top = min(MOBA_TOPK, nb - 1)
    hp = B_HEADS * B_HEAD_DIM // LANES
    outs = []
    for q0, q_len, kext in _causal_segments(s_len, MOBA_BLOCK):
        q_off, nqs, kbl = q0 // MOBA_BLOCK, q_len // MOBA_BLOCK, s_len // kext
        kern = functools.partial(_moba_kernel, n_top=n_top, s_len=kext, q_off=q_off)
        k_map = lambda b, g, i, kbl=kbl: (b * kbl, g)
        outs.append(pl.pallas_call(
            kern, grid=(batch, hp, nqs),
            in_specs=[pl.BlockSpec((MOBA_BLOCK, LANES), lambda b, g, i, q_off=q_off: (b * nb + q_off + i, g)),
                      pl.BlockSpec((kext, LANES), k_map),
                      pl.BlockSpec((kext, LANES), k_map)],
            out_specs=pl.BlockSpec((MOBA_BLOCK, LANES), lambda b, g, i, nqs=nqs: (b * nqs + i, g)),
            out_shape=jax.ShapeDtypeStruct((batch * q_len, B_HEADS * B_HEAD_DIM), F32),
            compiler_params=_params("parallel", "parallel", "arbitrary"), name="moba_attention",
        )(qb, kb, vb))
    return _join_segments(outs, batch)


MLA_GROUP = 4


def _mla_kernel(qn_ref, qr_ref, kn_ref, kr_ref, v_ref, o_ref, *, tq, s_len, q_off):
    i = pl.program_id(2) + q_off
    scale = (C_NOPE + C_ROPE) ** -0.5
    key_pos = lax.broadcasted_iota(I32, (tq, s_len), 1)
    q_pos = i * tq + lax.broadcasted_iota(I32, (tq, s_len), 0)
    causal = key_pos <= q_pos
    kr = kr_ref[:, 0:C_ROPE].astype(BF16)
    outs = []
    for hh in range(MLA_GROUP):
        qn = qn_ref[:, hh * C_NOPE:(hh + 1) * C_NOPE].astype(BF16)
        qr = qr_ref[:, hh * C_ROPE:(hh + 1) * C_ROPE].astype(BF16)
        kn = kn_ref[:, hh * C_NOPE:(hh + 1) * C_NOPE].astype(BF16)
        v = v_ref[:, hh * C_V:(hh + 1) * C_V].astype(BF16)
        s = (_dot_nt(qn, kn) + _dot_nt(qr, kr)) * scale
        outs.append(_softmax_pv(s, causal, v))
    o_ref[...] = jnp.concatenate(outs, axis=1)


def _mla_attention(qn, qr, kn, kr, v, batch, s_len, tq=256):
    nq = s_len // tq
    ng = C_HEADS // MLA_GROUP
    outs = []
    for q0, q_len, kext in _causal_segments(s_len, tq):
        q_off, nqs, kbl = q0 // tq, q_len // tq, s_len // kext
        kern = functools.partial(_mla_kernel, tq=tq, s_len=kext, q_off=q_off)
        q_map = lambda b, g, i, q_off=q_off: (b * nq + q_off + i, g)
        k_map = lambda b, g, i, kbl=kbl: (b * kbl, g)
        outs.append(pl.pallas_call(
            kern, grid=(batch, ng, nqs),
            in_specs=[pl.BlockSpec((tq, MLA_GROUP * C_NOPE), q_map),
                      pl.BlockSpec((tq, MLA_GROUP * C_ROPE), q_map),
                      pl.BlockSpec((kext, MLA_GROUP * C_NOPE), k_map),
                      pl.BlockSpec((kext, LANES), lambda b, g, i, kbl=kbl: (b * kbl, 0)),
                      pl.BlockSpec((kext, MLA_GROUP * C_V), k_map)],
            out_specs=pl.BlockSpec((tq, MLA_GROUP * C_V), lambda b, g, i, nqs=nqs: (b * nqs + i, g)),
            out_shape=jax.ShapeDtypeStruct((batch * q_len, C_HEADS * C_V), F32),
            compiler_params=_params("parallel", "parallel", "arbitrary"), name="mla_attention",
        )(qn, qr, kn, kr, v))
    return _join_segments(outs, batch)


def _topk_rows(sc, k):
    r = sc.shape[0]
    riota = lax.broadcasted_iota(I32, sc.shape, 0)
    vals, rows = [], []
    for _ in range(k):
        m = jnp.max(sc, axis=0, keepdims=True)
        pos = jnp.min(jnp.where(sc == m, riota, r), axis=0, keepdims=True)
        sc = jnp.where(riota == pos, -jnp.inf, sc)
        vals.append(m)
        rows.append(pos)
    return jnp.concatenate(vals, axis=0), jnp.concatenate(rows, axis=0)


def _peer_candidates():
    k = PEER_TOPK_HALF
    rows = [(0, b) for b in range(k)]
    rows += [(a, b) for a in range(1, 8) for b in range(8)]
    rows += [(a, 0) for a in range(8, k)]
    return np.array([a * k + b if (a + 1) * (b + 1) <= PEER_TOPK else -1 for a, b in rows], np.int32)


def _select_rows(tab, row):
    riota = lax.broadcasted_iota(I32, tab.shape, 0)
    return jnp.sum(jnp.where(riota == row, tab, 0), axis=0, keepdims=True)


def _peer_topk_kernel(q_ref, keys_ref, pos_ref, e_ref, g_ref, *, idx_offset):
    kh = PEER_TOPK_HALF
    pos_tab = pos_ref[...]
    big = kh * kh
    e_rows, g_rows = [], []
    for h in range(PEER_HEADS):
        half_v, half_i = [], []
        for p in range(2):
            c0 = (h * 2 + p) * PEER_NKEYS
            qh = q_ref[:, c0:c0 + PEER_NKEYS].astype(BF16)
            sc = _dot_nt(keys_ref[h, p], qh)
            v, ix = _topk_rows(sc, kh)
            half_v.append(v)
            half_i.append(ix)
        v0, v1 = half_v
        cand = jnp.concatenate(
            [v0[0:1] + v1] + [v0[a:a + 1] + v1[0:8] for a in range(1, 8)] + [v0[8:kh] + v1[0:1]], axis=0)
        cand = jnp.where(pos_tab >= 0, cand, -jnp.inf)
        pos_key = jnp.where(pos_tab >= 0, pos_tab, big)
        tops = []
        for _ in range(PEER_TOPK):
            m = jnp.max(cand, axis=0, keepdims=True)
            pos = jnp.min(jnp.where(cand == m, pos_key, big), axis=0, keepdims=True)
            cand = jnp.where(pos_key == pos, -jnp.inf, cand)
            tops.append(m)
            e_rows.append(_select_rows(half_i[0], pos // kh) * PEER_NKEYS
                          + _select_rows(half_i[1], pos % kh) + idx_offset)
        top_s = jnp.concatenate(tops, axis=0)
        ex = jnp.exp(top_s - top_s[0:1, :])
        g_rows.append(ex / jnp.sum(ex, axis=0, keepdims=True))
    e_ref[...] = jnp.concatenate(e_rows, axis=0).T
    g_ref[...] = jnp.concatenate(g_rows, axis=0).T


def _peer_topk(q, keys_bf, idx_offset, t=128):
    n = q.shape[0]
    pos_tab = jnp.asarray(np.tile(_peer_candidates()[:, None], (1, t)))
    out = pl.BlockSpec((t, PEER_SEL), lambda i: (i, 0))
    return pl.pallas_call(
        functools.partial(_peer_topk_kernel, idx_offset=idx_offset), grid=(n // t,),
        in_specs=[pl.BlockSpec((t, PEER_HEADS * PEER_QDIM), lambda i: (i, 0)),
                  pl.BlockSpec(keys_bf.shape, lambda i: (0, 0, 0, 0)),
                  pl.BlockSpec(pos_tab.shape, lambda i: (0, 0))],
        out_specs=[out, out],
        out_shape=[jax.ShapeDtypeStruct((n, PEER_SEL), I32), jax.ShapeDtypeStruct((n, PEER_SEL), F32)],
        compiler_params=_params("parallel"), name="peer_topk",
    )(q, keys_bf, pos_tab)


PEER_SEL = PEER_HEADS * PEER_TOPK

SC_CORES = 2
SC_SUBCORES = 16
SC_LANES = 16
SC_WORKERS = SC_CORES * SC_SUBCORES
SC_ROWS = 32
SC_TOK = 8
SC_CHUNKS = SC_TOK * PEER_SEL // SC_ROWS
SC_VCOLS = 256
SC_UNROLL = 1


def _sc_mesh():
    return plsc.VectorSubcoreMesh(core_axis_name="c", subcore_axis_name="s",
                                  num_cores=SC_CORES, num_subcores=SC_SUBCORES)


def _pack_table(t):
    half = t.shape[1] // 2
    b = lax.bitcast_convert_type(t.astype(BF16), jnp.uint16).astype(jnp.uint32)
    return lax.bitcast_convert_type(b[:, :half] | (b[:, half:] << 16), I32)


def _unpack_pair(words):
    lo = lax.bitcast_convert_type(jnp.left_shift(words, 16), F32)
    hi = lax.bitcast_convert_type(jnp.bitwise_and(words, jnp.int32(-65536)), F32)
    return lo, hi


def _sc_chunk_pipeline(gather, compute):
    gather(0, 0).start()

    @pl.loop(0, SC_CHUNKS, step=2)
    def _(q):
        gather(q + 1, 1).start()
        gather(q, 0).wait()
        compute(q, 0)

        @pl.when(q + 2 < SC_CHUNKS)
        def _():
            gather(q + 2, 0).start()

        gather(q + 1, 1).wait()
        compute(q + 1, 1)


def _sc_u_body(idx_hbm, h_hbm, u_hbm, act_hbm, idx_v, h_v, rows_v, act_v, sem0, sem1, *, tpw, d):
    tok_base = (lax.axis_index("s") * SC_CORES + lax.axis_index("c")) * tpw
    sems = (sem0, sem1)

    def gather(q, b):
        return pltpu.make_async_copy(u_hbm.at[idx_v.at[pl.ds(q * SC_ROWS, SC_ROWS)]], rows_v.at[b], sems[b])

    def compute(q, b):
        tok = q // (PEER_SEL // SC_ROWS)
        lane = lax.iota(I32, SC_LANES)
        for g in range(SC_ROWS // SC_LANES):
            def body(c, accs, g=g):
                x_lo = h_v[tok, pl.ds(c * SC_LANES, SC_LANES)]
                x_hi = h_v[tok, pl.ds(d // 2 + c * SC_LANES, SC_LANES)]
                out = []
                for r, a in enumerate(accs):
                    lo, hi = _unpack_pair(rows_v[b, g * SC_LANES + r, pl.ds(c * SC_LANES, SC_LANES)])
                    out.append(a + lo * x_lo + hi * x_hi)
                return tuple(out)
            accs = lax.fori_loop(0, d // (2 * SC_LANES), body,
                                 tuple(jnp.zeros((SC_LANES,), F32) for _ in range(SC_LANES)),
                                 unroll=SC_UNROLL)
            out = jnp.zeros((SC_LANES,), F32)
            for r in range(SC_LANES):
                out = jnp.where(lane == r, jnp.sum(accs[r]), out)
            act_v[pl.ds(q * SC_ROWS + g * SC_LANES, SC_LANES)] = out

    @pl.loop(0, tpw // SC_TOK)
    def _(blk):
        tok0 = tok_base + blk * SC_TOK
        pltpu.sync_copy(idx_hbm.at[pl.ds(tok0 * PEER_SEL, SC_TOK * PEER_SEL)], idx_v)
        pltpu.sync_copy(h_hbm.at[pl.ds(tok0, SC_TOK)], h_v)
        _sc_chunk_pipeline(gather, compute)
        pltpu.sync_copy(act_v, act_hbm.at[pl.ds(tok0 * PEER_SEL, SC_TOK * PEER_SEL)])


def _sc_peer_u(idx_flat, h, u_tab):
    n, d = h.shape
    assert u_tab.shape[1] * 2 == d and u_tab.dtype == I32
    assert n % (SC_WORKERS * SC_TOK) == 0 and d % SC_VCOLS == 0
    body = functools.partial(_sc_u_body, tpw=n // SC_WORKERS, d=d)
    return pl.kernel(
        body,
        out_type=jax.ShapeDtypeStruct((n * PEER_SEL,), F32),
        mesh=_sc_mesh(),
        scratch_types=[pltpu.VMEM((SC_TOK * PEER_SEL,), I32),
                       pltpu.VMEM((SC_TOK, d), F32),
                       pltpu.VMEM((2, SC_ROWS, d // 2), I32),
                       pltpu.VMEM((SC_TOK * PEER_SEL,), F32),
                       pltpu.SemaphoreType.DMA, pltpu.SemaphoreType.DMA],
        compiler_params=pltpu.CompilerParams(needs_layout_passes=False),
        name="sc_peer_u",
    )(idx_flat, h, u_tab)


def _sc_v_body(idx_hbm, w_hbm, x_hbm, v_hbm, o_hbm, idx_v, w_v, out_v, rows_v, sem0, sem1, *, tpw, d):
    tok_base = (lax.axis_index("s") * SC_CORES + lax.axis_index("c")) * tpw
    sems = (sem0, sem1)
    half = d // 2
    nword = SC_VCOLS // (2 * SC_LANES)

    def gather(q, b):
        return pltpu.make_async_copy(v_hbm.at[idx_v.at[pl.ds(q * SC_ROWS, SC_ROWS)]], rows_v.at[b], sems[b])

    def compute(q, b):
        tok = q // (PEER_SEL // SC_ROWS)
        for cp in range(d // SC_VCOLS):
            w0 = cp * nword * SC_LANES
            cols = [half * part + w0 + a * SC_LANES for a in range(nword) for part in range(2)]

            def body(r, accs, w0=w0):
                wv = plsc.load_gather(w_v, [jnp.full((SC_LANES,), q * SC_ROWS + r, I32)])
                out = []
                for a in range(nword):
                    lo, hi = _unpack_pair(rows_v[b, r, pl.ds(w0 + a * SC_LANES, SC_LANES)])
                    out += [accs[2 * a] + wv * lo, accs[2 * a + 1] + wv * hi]
                return tuple(out)

            accs = lax.fori_loop(0, SC_ROWS, body,
                                 tuple(out_v[tok, pl.ds(c, SC_LANES)] for c in cols))
            for c, acc in zip(cols, accs):
                out_v[tok, pl.ds(c, SC_LANES)] = acc

    @pl.loop(0, tpw // SC_TOK)
    def _(blk):
        tok0 = tok_base + blk * SC_TOK
        pltpu.sync_copy(idx_hbm.at[pl.ds(tok0 * PEER_SEL, SC_TOK * PEER_SEL)], idx_v)
        pltpu.sync_copy(w_hbm.at[pl.ds(tok0 * PEER_SEL, SC_TOK * PEER_SEL)], w_v)
        pltpu.sync_copy(x_hbm.at[pl.ds(tok0, SC_TOK)], out_v)
        _sc_chunk_pipeline(gather, compute)
        pltpu.sync_copy(out_v, o_hbm.at[pl.ds(tok0, SC_TOK)])


def _sc_peer_v(idx_flat, w_flat, x, v_tab):
    n, d = x.shape
    assert v_tab.shape[1] * 2 == d and v_tab.dtype == I32
    assert n % (SC_WORKERS * SC_TOK) == 0 and d % SC_VCOLS == 0
    body = functools.partial(_sc_v_body, tpw=n // SC_WORKERS, d=d)
    return pl.kernel(
        body,
        out_type=jax.ShapeDtypeStruct((n, d), F32),
        mesh=_sc_mesh(),
        scratch_types=[pltpu.VMEM((SC_TOK * PEER_SEL,), I32),
                       pltpu.VMEM((SC_TOK * PEER_SEL,), F32),
                       pltpu.VMEM((SC_TOK, d), F32),
                       pltpu.VMEM((2, SC_ROWS, d // 2), I32),
                       pltpu.SemaphoreType.DMA, pltpu.SemaphoreType.DMA],
        compiler_params=pltpu.CompilerParams(needs_layout_passes=False),
        name="sc_peer_v",
    )(idx_flat, w_flat, x, v_tab)


def _peer_gate_kernel(g_ref, act_ref, w_ref):
    act = act_ref[...]
    w_ref[...] = g_ref[...] * (0.5 * act * (1.0 + lax.erf(act * (2.0 ** -0.5))))


def _peer_gate(g, act, tm=2048):
    n = g.shape[0]
    spec = pl.BlockSpec((tm, PEER_SEL), lambda i: (i, 0))
    return pl.pallas_call(
        _peer_gate_kernel, grid=(n // tm,), in_specs=[spec, spec], out_specs=spec,
        out_shape=jax.ShapeDtypeStruct((n, PEER_SEL), F32),
        compiler_params=_params("parallel"), name="peer_gate",
    )(g, act)


def _pad_cols(w, width):
    return jnp.pad(w, ((0, 0), (0, width - w.shape[1])))


def _even_weights(w_in):
    hd = A_HEADS * A_HEAD_DIM
    cols = np.cumsum([0, hd, A_HEAD_DIM, A_HEAD_DIM, IDX_HEADS * IDX_DIM, IDX_DIM, IDX_HEADS,
                      B_HEADS * B_HEAD_DIM, B_HEADS * B_HEAD_DIM, B_HEADS * B_HEAD_DIM])
    qa, ka, va, iq, ik, iw, qb, kb, vb = [w_in[:, cols[j]:cols[j + 1]] for j in range(9)]
    vaiw = _pad_cols(jnp.concatenate([va, iw], axis=1), LANES)
    w = jnp.concatenate([qa, iq, qb, kb, vb, ka, ik, vaiw], axis=1).astype(BF16)
    groups = [(0, 512, "r64"), (512, 512, "r64"), (1024, 512, "r64"), (1536, 512, "r64"),
              (2048, 512, None), (2560, LANES, "r64"), (2560 + LANES, LANES, None)]
    return w, groups


def _even_mixer(x, gain, w_in, w_out, tables, batch, s_len):
    w, groups = _even_weights(w_in)
    qa, iq, qb, kb, vb, kaik, vaiw = _fused_proj(x, gain, w, groups, tables)
    out_a = _dsa_attention(iq, qa, kaik, vaiw, batch, s_len)
    out_b = _moba_attention(qb, kb, vb, batch, s_len)
    hd = A_HEADS * A_HEAD_DIM
    w_out = w_out.astype(BF16)
    return _outproj_residual(x, [out_a, out_b], [w_out[:hd], w_out[hd:]])


def _mla_mixer(x, gain, w_in, q_norm, kv_norm, w_uq, w_ukv, w_out, tables, batch, s_len):
    w1 = _pad_cols(w_in, C_Q_RANK + C_KV_RANK + LANES).astype(BF16)
    groups1 = [(0, C_Q_RANK, None), (C_Q_RANK, C_KV_RANK, None), (C_Q_RANK + C_KV_RANK, LANES, "r32")]
    cq, ckv, kr = _fused_proj(x, gain, w1, groups1, tables)
    qd = C_NOPE + C_ROPE
    nope_cols = np.concatenate([np.arange(h * qd, h * qd + C_NOPE) for h in range(C_HEADS)])
    rope_cols = np.concatenate([np.arange(h * qd + C_NOPE, (h + 1) * qd) for h in range(C_HEADS)])
    w2 = jnp.concatenate([w_uq[:, nope_cols], w_uq[:, rope_cols]], axis=1).astype(BF16)
    groups2 = [(0, C_HEADS * C_NOPE, None), (C_HEADS * C_NOPE, C_HEADS * C_ROPE, "r32")]
    qn, qr = _fused_proj(cq, q_norm, w2, groups2, tables)
    kd = C_NOPE + C_V
    kn_cols = np.concatenate([np.arange(h * kd, h * kd + C_NOPE) for h in range(C_HEADS)])
    v_cols = np.concatenate([np.arange(h * kd + C_NOPE, (h + 1) * kd) for h in range(C_HEADS)])
    w3 = jnp.concatenate([w_ukv[:, kn_cols], w_ukv[:, v_cols]], axis=1).astype(BF16)
    groups3 = [(0, C_HEADS * C_NOPE, None), (C_HEADS * C_NOPE, C_HEADS * C_V, None)]
    kn, v = _fused_proj(ckv, kv_norm, w3, groups3, tables)
    o = _mla_attention(qn, qr, kn, kr, v, batch, s_len)
    return _outproj_residual(x, [o], [w_out.astype(BF16)])


def _peer_ffn(x, gain, w_q, sub_keys, u_all, v_all, layer):
    n, d = x.shape
    n_experts = PEER_NKEYS * PEER_NKEYS
    q, h = _fused_proj(x, gain, w_q.astype(BF16), [(0, PEER_HEADS * PEER_QDIM, None)], {}, emit_h=True)
    idx, g = _peer_topk(q, sub_keys.astype(BF16), layer * n_experts)
    idx_flat = idx.reshape(n * PEER_SEL)
    act = _sc_peer_u(idx_flat, h, u_all)
    w = _peer_gate(g, act.reshape(n, PEER_SEL))
    return _sc_peer_v(idx_flat, w.reshape(n * PEER_SEL), x, v_all)


def kernel(x, positions, attn_norm, ffn_norm, final_norm, hyb_w_in, hyb_w_out, mla_w_in, mla_q_norm,
           mla_kv_norm, mla_w_uq, mla_w_ukv, mla_w_out, peer_w_q, peer_sub_keys, peer_u, peer_v):
    batch, s_len, d = x.shape
    depth = attn_norm.shape[0]
    u_all = _pack_table(peer_u.reshape(-1, d))
    v_all = _pack_table(peer_v.reshape(-1, d))
    gb = batch // BATCH_GROUPS
    gn = gb * s_len
    xs, tabs = [], []
    for g in range(BATCH_GROUPS):
        pos = positions[g * gb:(g + 1) * gb].reshape(gn, 1).astype(I32)
        tabs.append({"r64": _rope_tables(pos, A_HEAD_DIM), "r32": _rope_tables(pos, C_ROPE)})
        xs.append(x[g * gb:(g + 1) * gb].reshape(gn, d))
    order = sorted(((i, g) for i in range(depth) for g in range(BATCH_GROUPS)),
                   key=lambda t: (t[0] + t[1], -t[0]))
    for i, g in order:
        j = i // 2
        if i % 2 == 0:
            xs[g] = _even_mixer(xs[g], attn_norm[i], hyb_w_in[j], hyb_w_out[j], tabs[g], gb, s_len)
        else:
            xs[g] = _mla_mixer(xs[g], attn_norm[i], mla_w_in[j], mla_q_norm[j], mla_kv_norm[j],
                               mla_w_uq[j], mla_w_ukv[j], mla_w_out[j], tabs[g], gb, s_len)
        xs[g] = _peer_ffn(xs[g], ffn_norm[i], peer_w_q[i], peer_sub_keys[i], u_all, v_all, i)
    outs = [_rmsnorm(xg, final_norm).reshape(gb, s_len, d) for xg in xs]
    return jnp.concatenate(outs, axis=0)
```

```python
import functools

import numpy as np
import jax
import jax.numpy as jnp
from jax import lax
from jax.experimental import pallas as pl
from jax.experimental.pallas import tpu as pltpu
from jax.experimental.pallas import tpu_sc as plsc

NORM_EPS = 1e-6
ROPE_THETA = 10000.0

A_HEADS = 8
A_HEAD_DIM = 64
IDX_HEADS = 8
IDX_DIM = 64
DSA_TOPK_MAX = 256

B_HEADS = 8
B_HEAD_DIM = 64
MOBA_BLOCK = 256
MOBA_TOPK = 3

C_HEADS = 16
C_NOPE = 64
C_ROPE = 32
C_V = 64
C_Q_RANK = 768
C_KV_RANK = 256

PEER_HEADS = 8
PEER_NKEYS = 128
PEER_QDIM = 256
PEER_TOPK_HALF = 16
PEER_TOPK = 16

BATCH_GROUPS = 8
ATTN_MAX_SPLIT = 4
LANES = 128
VMEM_LIMIT = 56 * 1024 * 1024

F32 = jnp.float32
BF16 = jnp.bfloat16
I32 = jnp.int32
INT_MIN = -(2 ** 31)


def _params(*sem):
    return pltpu.CompilerParams(dimension_semantics=sem, vmem_limit_bytes=VMEM_LIMIT)


def _dot(a, b):
    return jnp.dot(a, b, preferred_element_type=F32)


def _dot_nt(a, b):
    return lax.dot_general(a, b, (((1,), (1,)), ((), ())), preferred_element_type=F32)


def _rope_table_kernel(pos_ref, inv_ref, sign_ref, cos_ref, sin_ref):
    ang = pos_ref[...].astype(F32) * inv_ref[...]
    cos_ref[...] = jnp.cos(ang)
    sin_ref[...] = jnp.sin(ang) * sign_ref[...]


def _rope_tables(pos, dim, tm=1024):
    n = pos.shape[0]
    half = dim // 2
    inv = 1.0 / (ROPE_THETA ** (jnp.arange(0, dim, 2, dtype=F32) / dim))
    reps = LANES // dim
    inv_l = jnp.tile(jnp.concatenate([inv, inv]), reps)[None, :]
    sign_l = jnp.tile(jnp.concatenate([-jnp.ones(half, F32), jnp.ones(half, F32)]), reps)[None, :]
    out = jax.ShapeDtypeStruct((n, LANES), F32)
    return pl.pallas_call(
        _rope_table_kernel,
        grid=(n // tm,),
        in_specs=[pl.BlockSpec((tm, 1), lambda i: (i, 0)),
                  pl.BlockSpec((1, LANES), lambda i: (0, 0)),
                  pl.BlockSpec((1, LANES), lambda i: (0, 0))],
        out_specs=[pl.BlockSpec((tm, LANES), lambda i: (i, 0))] * 2,
        out_shape=[out, out],
        compiler_params=_params("parallel"),
        name="rope_tables",
    )(pos, inv_l, sign_l)


def _rope_epilogue(acc, cos, sin, half):
    wd = acc.shape[1]
    reps = wd // LANES
    if reps > 1:
        cos = jnp.concatenate([cos] * reps, axis=1)
        sin = jnp.concatenate([sin] * reps, axis=1)
    lane = lax.broadcasted_iota(I32, acc.shape, 1)
    first = (lane % (2 * half)) < half
    partner = jnp.where(first, pltpu.roll(acc, wd - half, 1), pltpu.roll(acc, half, 1))
    return acc * cos + partner * sin


def _proj_kernel(*refs, groups, has_gain, emit_h, tab_names):
    it = iter(refs)
    x_ref = next(it)
    g_ref = next(it) if has_gain else None
    w_ref = next(it)
    tabs = {name: (next(it), next(it)) for name in tab_names}
    out_refs = [next(it) for _ in groups]
    h_ref = next(it) if emit_h else None

    x = x_ref[...]
    if has_gain:
        ms = jnp.mean(x * x, axis=-1, keepdims=True)
        h = x * lax.rsqrt(ms + NORM_EPS) * g_ref[...]
    else:
        h = x
    if emit_h:
        h_ref[...] = h
    hb = h.astype(BF16)
    for o_ref, (c0, wd, rope) in zip(out_refs, groups):
        acc = _dot(hb, w_ref[:, c0:c0 + wd])
        if rope is not None:
            cos_ref, sin_ref = tabs[rope]
            acc = _rope_epilogue(acc, cos_ref[...], sin_ref[...], 32 if rope == "r64" else 16)
        o_ref[...] = acc


def _fused_proj(x, gain, w, groups, tables, emit_h=False, tm=256):
    n, k = x.shape
    tab_names = sorted({g[2] for g in groups if g[2] is not None})
    args = [x]
    in_specs = [pl.BlockSpec((tm, k), lambda i: (i, 0))]
    if gain is not None:
        args.append(gain.reshape(1, k).astype(F32))
        in_specs.append(pl.BlockSpec((1, k), lambda i: (0, 0)))
    args.append(w)
    in_specs.append(pl.BlockSpec(w.shape, lambda i: (0, 0)))
    for name in tab_names:
        for t in tables[name]:
            args.append(t)
            in_specs.append(pl.BlockSpec((tm, LANES), lambda i: (i, 0)))
    out_shape = [jax.ShapeDtypeStruct((n, wd), F32) for (_, wd, _) in groups]
    out_specs = [pl.BlockSpec((tm, wd), lambda i: (i, 0)) for (_, wd, _) in groups]
    if emit_h:
        out_shape.append(jax.ShapeDtypeStruct((n, k), F32))
        out_specs.append(pl.BlockSpec((tm, k), lambda i: (i, 0)))
    kern = functools.partial(_proj_kernel, groups=tuple(groups), has_gain=gain is not None,
                             emit_h=emit_h, tab_names=tuple(tab_names))
    return pl.pallas_call(
        kern, grid=(n // tm,), in_specs=in_specs, out_specs=out_specs, out_shape=out_shape,
        compiler_params=_params("parallel"), name="fused_proj",
    )(*args)


def _outproj_kernel(*refs, n_in):
    res_ref = refs[0]
    a_refs = refs[1:1 + n_in]
    w_refs = refs[1 + n_in:1 + 2 * n_in]
    o_ref = refs[-1]
    acc = res_ref[...]
    for a_ref, w_ref in zip(a_refs, w_refs):
        acc = acc + _dot(a_ref[...].astype(BF16), w_ref[...])
    o_ref[...] = acc


def _outproj_residual(res, acts, ws, tm=256):
    n, d = res.shape
    in_specs = [pl.BlockSpec((tm, d), lambda i: (i, 0))]
    in_specs += [pl.BlockSpec((tm, a.shape[1]), lambda i: (i, 0)) for a in acts]
    in_specs += [pl.BlockSpec(w.shape, lambda i: (0, 0)) for w in ws]
    return pl.pallas_call(
        functools.partial(_outproj_kernel, n_in=len(acts)),
        grid=(n // tm,), in_specs=in_specs,
        out_specs=pl.BlockSpec((tm, d), lambda i: (i, 0)),
        out_shape=jax.ShapeDtypeStruct((n, d), F32),
        compiler_params=_params("parallel"), name="outproj_residual",
    )(res, *acts, *ws)


def _rmsnorm_kernel(x_ref, g_ref, o_ref):
    x = x_ref[...]
    ms = jnp.mean(x * x, axis=-1, keepdims=True)
    o_ref[...] = x * lax.rsqrt(ms + NORM_EPS) * g_ref[...]


def _rmsnorm(x, gain, tm=512):
    n, d = x.shape
    return pl.pallas_call(
        _rmsnorm_kernel, grid=(n // tm,),
        in_specs=[pl.BlockSpec((tm, d), lambda i: (i, 0)), pl.BlockSpec((1, d), lambda i: (0, 0))],
        out_specs=pl.BlockSpec((tm, d), lambda i: (i, 0)),
        out_shape=jax.ShapeDtypeStruct((n, d), F32),
        compiler_params=_params("parallel"), name="final_rmsnorm",
    )(x, gain.reshape(1, d))


def _softmax_pv(s, mask, v_bf):
    s = jnp.where(mask, s, -jnp.inf)
    m = jnp.max(s, axis=-1, keepdims=True)
    p = jnp.exp(s - m)
    l = jnp.sum(p, axis=-1, keepdims=True)
    return _dot(p.astype(BF16), v_bf) / l


def _dsa_kernel(iq_ref, qa_ref, kaik_ref, va_ref, iw_ref, o_ref, *, n_sel, tq, s_len, q_off):
    i = pl.program_id(1) + q_off
    kaik = kaik_ref[...]
    ka = kaik[:, 0:A_HEAD_DIM].astype(BF16)
    ik = kaik[:, A_HEAD_DIM:A_HEAD_DIM + IDX_DIM].astype(BF16)
    va = va_ref[:, 0:A_HEAD_DIM].astype(BF16)
    iw = iw_ref[:, A_HEAD_DIM:A_HEAD_DIM + IDX_HEADS]
    iq = iq_ref[...]

    score = jnp.zeros((tq, s_len), F32)
    for h in range(IDX_HEADS):
        lg = _dot_nt(iq[:, h * IDX_DIM:(h + 1) * IDX_DIM].astype(BF16), ik)
        score = score + jnp.maximum(lg, 0.0) * iw[:, h:h + 1]
    w_scale = (IDX_HEADS ** -0.5) * (IDX_DIM ** -0.5)
    score = score * w_scale + 0.0

    key_pos = lax.broadcasted_iota(I32, (tq, s_len), 1)
    q_pos = i * tq + lax.broadcasted_iota(I32, (tq, s_len), 0)
    causal = key_pos <= q_pos

    bits = pltpu.bitcast(score, I32)
    key = jnp.where(bits < 0, bits ^ 0x7FFFFFFF, bits)
    key = jnp.where(causal, key, INT_MIN)

    def count_ge(cand):
        return jnp.sum(jnp.where(key >= cand, 1.0, 0.0), axis=-1, keepdims=True)

    lo = jnp.where(count_ge(jnp.zeros((tq, 1), I32)) >= n_sel, 0, INT_MIN).astype(I32)

    def bisect(t, lo):
        cand = lo + jnp.left_shift(jnp.int32(1), 30 - t)
        return jnp.where(count_ge(cand) >= n_sel, cand, lo)

    thr = lax.fori_loop(0, 31, bisect, lo)

    gt = key > thr
    eq = key == thr
    need = n_sel - jnp.sum(jnp.where(gt, 1.0, 0.0), axis=-1, keepdims=True)
    r = lax.broadcasted_iota(I32, (LANES, LANES), 0)
    c = lax.broadcasted_iota(I32, (LANES, LANES), 1)
    tri = jnp.where(r < c, 1.0, 0.0).astype(BF16)
    eq_bf = jnp.where(eq, 1.0, 0.0).astype(BF16)
    carry = jnp.zeros((tq, 1), F32)
    pref = []
    for cb in range(s_len // LANES):
        blk = eq_bf[:, cb * LANES:(cb + 1) * LANES]
        pref.append(_dot(blk, tri) + carry)
        carry = carry + jnp.sum(blk.astype(F32), axis=-1, keepdims=True)
    prefix = jnp.concatenate(pref, axis=1)
    sel = (gt | (eq & (prefix < need))) & causal

    a_scale = A_HEAD_DIM ** -0.5
    qa = qa_ref[...]
    outs = []
    for h in range(A_HEADS):
        sc = _dot_nt(qa[:, h * A_HEAD_DIM:(h + 1) * A_HEAD_DIM].astype(BF16), ka) * a_scale
        outs.append(_softmax_pv(sc, sel, va))
    o_ref[...] = jnp.concatenate(outs, axis=1)


def _causal_segments(s_len, unit):
    ext = s_len
    while s_len // ext < ATTN_MAX_SPLIT and (ext // 2) % unit == 0:
        ext //= 2
    segs = [(0, ext, ext)]
    while ext < s_len:
        segs.append((ext, ext, 2 * ext))
        ext *= 2
    return segs


def _join_segments(outs, batch):
    return jnp.concatenate([o.reshape(batch, -1, o.shape[-1]) for o in outs], axis=1).reshape(-1, outs[0].shape[-1])


def _dsa_attention(iq, qa, kaik, vaiw, batch, s_len, tq=128):
    nq = s_len // tq
    n_sel = min(DSA_TOPK_MAX, s_len // 4)
    hd = A_HEADS * A_HEAD_DIM
    outs = []
    for q0, q_len, kext in _causal_segments(s_len, tq):
        q_off, nqs, kb = q0 // tq, q_len // tq, s_len // kext
        kern = functools.partial(_dsa_kernel, n_sel=n_sel, tq=tq, s_len=kext, q_off=q_off)
        q_map = lambda b, i, q_off=q_off: (b * nq + q_off + i, 0)
        k_map = lambda b, i, kb=kb: (b * kb, 0)
        outs.append(pl.pallas_call(
            kern, grid=(batch, nqs),
            in_specs=[pl.BlockSpec((tq, IDX_HEADS * IDX_DIM), q_map),
                      pl.BlockSpec((tq, hd), q_map),
                      pl.BlockSpec((kext, LANES), k_map),
                      pl.BlockSpec((kext, LANES), k_map),
                      pl.BlockSpec((tq, LANES), q_map)],
            out_specs=pl.BlockSpec((tq, hd), lambda b, i, nqs=nqs: (b * nqs + i, 0)),
            out_shape=jax.ShapeDtypeStruct((batch * q_len, hd), F32),
            compiler_params=_params("parallel", "arbitrary"), name="dsa_attention",
        )(iq, qa, kaik, vaiw, vaiw))
    return _join_segments(outs, batch)


def _moba_kernel(q_ref, k_ref, v_ref, o_ref, *, n_top, s_len, q_off):
    i = pl.program_id(2) + q_off
    tq = MOBA_BLOCK
    nb = s_len // MOBA_BLOCK
    d = B_HEAD_DIM
    scale = d ** -0.5
    key_pos = lax.broadcasted_iota(I32, (tq, s_len), 1)
    q_pos = i * tq + lax.broadcasted_iota(I32, (tq, s_len), 0)
    own_mask = (key_pos >= i * MOBA_BLOCK) & (key_pos <= q_pos)
    blk_lane = lax.broadcasted_iota(I32, (tq, nb), 1)
    expand = jnp.where(lax.broadcasted_iota(I32, (nb, s_len), 1) // MOBA_BLOCK
                       == lax.broadcasted_iota(I32, (nb, s_len), 0), 1.0, 0.0).astype(BF16)
    outs = []
    for hh in range(LANES // d):
        q = q_ref[:, hh * d:(hh + 1) * d].astype(BF16)
        k = k_ref[:, hh * d:(hh + 1) * d]
        v = v_ref[:, hh * d:(hh + 1) * d].astype(BF16)
        k_mean = jnp.sum(k.reshape(nb, MOBA_BLOCK, d), axis=1) / MOBA_BLOCK
        gate = _dot_nt(q, k_mean.astype(BF16))
        gate = jnp.where(blk_lane < i, gate, -jnp.inf)
        rank = jnp.zeros((tq, nb), F32)
        for b2 in range(nb):
            gb = gate[:, b2:b2 + 1]
            beats = (gb > gate) | ((gb == gate) & (b2 < blk_lane))
            rank = rank + jnp.where(beats, 1.0, 0.0)
        sel = jnp.where((rank < n_top) & (blk_lane < i), 1.0, 0.0).astype(BF16)
        selk = _dot(sel, expand)
        mask = (selk > 0.5) | own_mask
        s = _dot_nt(q, k.astype(BF16)) * scale
        outs.append(_softmax_pv(s, mask, v))
    o_ref[...] = jnp.concatenate(outs, axis=1)


def _moba_attention(qb, kb, vb, batch, s_len):
    nb = s_len // MOBA_BLOCK
    n_top = min(MOBA_TOPK, nb - 1)
    hp = B_HEADS * B_HEAD_DIM // LANES
    outs = []
    for q0, q_len, kext in _causal_segments(s_len, MOBA_BLOCK):
        q_off, nqs, kbl = q0 // MOBA_BLOCK, q_len // MOBA_BLOCK, s_len // kext
        kern = functools.partial(_moba_kernel, n_top=n_top, s_len=kext, q_off=q_off)
        k_map = lambda b, g, i, kbl=kbl: (b * kbl, g)
        outs.append(pl.pallas_call(
            kern, grid=(batch, hp, nqs),
            in_specs=[pl.BlockSpec((MOBA_BLOCK, LANES), lambda b, g, i, q_off=q_off: (b * nb + q_off + i, g)),
                      pl.BlockSpec((kext, LANES), k_map),
                      pl.BlockSpec((kext, LANES), k_map)],
            out_specs=pl.BlockSpec((MOBA_BLOCK, LANES), lambda b, g, i, nqs=nqs: (b * nqs + i, g)),
            out_shape=jax.ShapeDtypeStruct((batch * q_len, B_HEADS * B_HEAD_DIM), F32),
            compiler_params=_params("parallel", "parallel", "arbitrary"), name="moba_attention",
        )(qb, kb, vb))
    return _join_segments(outs, batch)


MLA_GROUP = 4


def _mla_kernel(qn_ref, qr_ref, kn_ref, kr_ref, v_ref, o_ref, *, tq, s_len, q_off):
    i = pl.program_id(2) + q_off
    scale = (C_NOPE + C_ROPE) ** -0.5
    key_pos = lax.broadcasted_iota(I32, (tq, s_len), 1)
    q_pos = i * tq + lax.broadcasted_iota(I32, (tq, s_len), 0)
    causal = key_pos <= q_pos
    kr = kr_ref[:, 0:C_ROPE].astype(BF16)
    outs = []
    for hh in range(MLA_GROUP):
        qn = qn_ref[:, hh * C_NOPE:(hh + 1) * C_NOPE].astype(BF16)
        qr = qr_ref[:, hh * C_ROPE:(hh + 1) * C_ROPE].astype(BF16)
        kn = kn_ref[:, hh * C_NOPE:(hh + 1) * C_NOPE].astype(BF16)
        v = v_ref[:, hh * C_V:(hh + 1) * C_V].astype(BF16)
        s = (_dot_nt(qn, kn) + _dot_nt(qr, kr)) * scale
        outs.append(_softmax_pv(s, causal, v))
    o_ref[...] = jnp.concatenate(outs, axis=1)


def _mla_attention(qn, qr, kn, kr, v, batch, s_len, tq=256):
    nq = s_len // tq
    ng = C_HEADS // MLA_GROUP
    outs = []
    for q0, q_len, kext in _causal_segments(s_len, tq):
        q_off, nqs, kbl = q0 // tq, q_len // tq, s_len // kext
        kern = functools.partial(_mla_kernel, tq=tq, s_len=kext, q_off=q_off)
        q_map = lambda b, g, i, q_off=q_off: (b * nq + q_off + i, g)
        k_map = lambda b, g, i, kbl=kbl: (b * kbl, g)
        outs.append(pl.pallas_call(
            kern, grid=(batch, ng, nqs),
            in_specs=[pl.BlockSpec((tq, MLA_GROUP * C_NOPE), q_map),
                      pl.BlockSpec((tq, MLA_GROUP * C_ROPE), q_map),
                      pl.BlockSpec((kext, MLA_GROUP * C_NOPE), k_map),
                      pl.BlockSpec((kext, LANES), lambda b, g, i, kbl=kbl: (b * kbl, 0)),
                      pl.BlockSpec((kext, MLA_GROUP * C_V), k_map)],
            out_specs=pl.BlockSpec((tq, MLA_GROUP * C_V), lambda b, g, i, nqs=nqs: (b * nqs + i, g)),
            out_shape=jax.ShapeDtypeStruct((batch * q_len, C_HEADS * C_V), F32),
            compiler_params=_params("parallel", "parallel", "arbitrary"), name="mla_attention",
        )(qn, qr, kn, kr, v))
    return _join_segments(outs, batch)


def _topk_rows(sc, k):
    r = sc.shape[0]
    riota = lax.broadcasted_iota(I32, sc.shape, 0)
    vals, rows = [], []
    for _ in range(k):
        m = jnp.max(sc, axis=0, keepdims=True)
        pos = jnp.min(jnp.where(sc == m, riota, r), axis=0, keepdims=True)
        sc = jnp.where(riota == pos, -jnp.inf, sc)
        vals.append(m)
        rows.append(pos)
    return jnp.concatenate(vals, axis=0), jnp.concatenate(rows, axis=0)


def _peer_candidates():
    k = PEER_TOPK_HALF
    rows = [(0, b) for b in range(k)]
    rows += [(a, b) for a in range(1, 8) for b in range(8)]
    rows += [(a, 0) for a in range(8, k)]
    return np.array([a * k + b if (a + 1) * (b + 1) <= PEER_TOPK else -1 for a, b in rows], np.int32)


def _select_rows(tab, row):
    riota = lax.broadcasted_iota(I32, tab.shape, 0)
    return jnp.sum(jnp.where(riota == row, tab, 0), axis=0, keepdims=True)


def _peer_topk_kernel(q_ref, keys_ref, pos_ref, e_ref, g_ref, *, idx_offset):
    kh = PEER_TOPK_HALF
    pos_tab = pos_ref[...]
    big = kh * kh
    e_rows, g_rows = [], []
    for h in range(PEER_HEADS):
        half_v, half_i = [], []
        for p in range(2):
            c0 = (h * 2 + p) * PEER_NKEYS
            qh = q_ref[:, c0:c0 + PEER_NKEYS].astype(BF16)
            sc = _dot_nt(keys_ref[h, p], qh)
            v, ix = _topk_rows(sc, kh)
            half_v.append(v)
            half_i.append(ix)
        v0, v1 = half_v
        cand = jnp.concatenate(
            [v0[0:1] + v1] + [v0[a:a + 1] + v1[0:8] for a in range(1, 8)] + [v0[8:kh] + v1[0:1]], axis=0)
        cand = jnp.where(pos_tab >= 0, cand, -jnp.inf)
        pos_key = jnp.where(pos_tab >= 0, pos_tab, big)
        tops = []
        for _ in range(PEER_TOPK):
            m = jnp.max(cand, axis=0, keepdims=True)
            pos = jnp.min(jnp.where(cand == m, pos_key, big), axis=0, keepdims=True)
            cand = jnp.where(pos_key == pos, -jnp.inf, cand)
            tops.append(m)
            e_rows.append(_select_rows(half_i[0], pos // kh) * PEER_NKEYS
                          + _select_rows(half_i[1], pos % kh) + idx_offset)
        top_s = jnp.concatenate(tops, axis=0)
        ex = jnp.exp(top_s - top_s[0:1, :])
        g_rows.append(ex / jnp.sum(ex, axis=0, keepdims=True))
    e_ref[...] = jnp.concatenate(e_rows, axis=0).T
    g_ref[...] = jnp.concatenate(g_rows, axis=0).T


def _peer_topk(q, keys_bf, idx_offset, t=128):
    n = q.shape[0]
    pos_tab = jnp.asarray(np.tile(_peer_candidates()[:, None], (1, t)))
    out = pl.BlockSpec((t, PEER_SEL), lambda i: (i, 0))
    return pl.pallas_call(
        functools.partial(_peer_topk_kernel, idx_offset=idx_offset), grid=(n // t,),
        in_specs=[pl.BlockSpec((t, PEER_HEADS * PEER_QDIM), lambda i: (i, 0)),
                  pl.BlockSpec(keys_bf.shape, lambda i: (0, 0, 0, 0)),
                  pl.BlockSpec(pos_tab.shape, lambda i: (0, 0))],
        out_specs=[out, out],
        out_shape=[jax.ShapeDtypeStruct((n, PEER_SEL), I32), jax.ShapeDtypeStruct((n, PEER_SEL), F32)],
        compiler_params=_params("parallel"), name="peer_topk",
    )(q, keys_bf, pos_tab)


PEER_SEL = PEER_HEADS * PEER_TOPK

SC_CORES = 2
SC_SUBCORES = 16
SC_LANES = 16
SC_WORKERS = SC_CORES * SC_SUBCORES
SC_ROWS = 32
SC_TOK = 8
SC_CHUNKS = SC_TOK * PEER_SEL // SC_ROWS
SC_VCOLS = 256
SC_UNROLL = 1


def _sc_mesh():
    return plsc.VectorSubcoreMesh(core_axis_name="c", subcore_axis_name="s",
                                  num_cores=SC_CORES, num_subcores=SC_SUBCORES)


def _pack_table(t):
    half = t.shape[1] // 2
    b = lax.bitcast_convert_type(t.astype(BF16), jnp.uint16).astype(jnp.uint32)
    return lax.bitcast_convert_type(b[:, :half] | (b[:, half:] << 16), I32)


def _unpack_pair(words):
    lo = lax.bitcast_convert_type(jnp.left_shift(words, 16), F32)
    hi = lax.bitcast_convert_type(jnp.bitwise_and(words, jnp.int32(-65536)), F32)
    return lo, hi


def _sc_chunk_pipeline(gather, compute):
    gather(0, 0).start()

    @pl.loop(0, SC_CHUNKS, step=2)
    def _(q):
        gather(q + 1, 1).start()
        gather(q, 0).wait()
        compute(q, 0)

        @pl.when(q + 2 < SC_CHUNKS)
        def _():
            gather(q + 2, 0).start()

        gather(q + 1, 1).wait()
        compute(q + 1, 1)


def _sc_u_body(idx_hbm, h_hbm, u_hbm, act_hbm, idx_v, h_v, rows_v, act_v, sem0, sem1, *, tpw, d):
    tok_base = (lax.axis_index("s") * SC_CORES + lax.axis_index("c")) * tpw
    sems = (sem0, sem1)

    def gather(q, b):
        return pltpu.make_async_copy(u_hbm.at[idx_v.at[pl.ds(q * SC_ROWS, SC_ROWS)]], rows_v.at[b], sems[b])

    def compute(q, b):
        tok = q // (PEER_SEL // SC_ROWS)
        lane = lax.iota(I32, SC_LANES)
        for g in range(SC_ROWS // SC_LANES):
            def body(c, accs, g=g):
                x_lo = h_v[tok, pl.ds(c * SC_LANES, SC_LANES)]
                x_hi = h_v[tok, pl.ds(d // 2 + c * SC_LANES, SC_LANES)]
                out = []
                for r, a in enumerate(accs):
                    lo, hi = _unpack_pair(rows_v[b, g * SC_LANES + r, pl.ds(c * SC_LANES, SC_LANES)])
                    out.append(a + lo * x_lo + hi * x_hi)
                return tuple(out)
            accs = lax.fori_loop(0, d // (2 * SC_LANES), body,
                                 tuple(jnp.zeros((SC_LANES,), F32) for _ in range(SC_LANES)),
                                 unroll=SC_UNROLL)
            out = jnp.zeros((SC_LANES,), F32)
            for r in range(SC_LANES):
                out = jnp.where(lane == r, jnp.sum(accs[r]), out)
            act_v[pl.ds(q * SC_ROWS + g * SC_LANES, SC_LANES)] = out

    @pl.loop(0, tpw // SC_TOK)
    def _(blk):
        tok0 = tok_base + blk * SC_TOK
        pltpu.sync_copy(idx_hbm.at[pl.ds(tok0 * PEER_SEL, SC_TOK * PEER_SEL)], idx_v)
        pltpu.sync_copy(h_hbm.at[pl.ds(tok0, SC_TOK)], h_v)
        _sc_chunk_pipeline(gather, compute)
        pltpu.sync_copy(act_v, act_hbm.at[pl.ds(tok0 * PEER_SEL, SC_TOK * PEER_SEL)])


def _sc_peer_u(idx_flat, h, u_tab):
    n, d = h.shape
    assert u_tab.shape[1] * 2 == d and u_tab.dtype == I32
    assert n % (SC_WORKERS * SC_TOK) == 0 and d % SC_VCOLS == 0
    body = functools.partial(_sc_u_body, tpw=n // SC_WORKERS, d=d)
    return pl.kernel(
        body,
        out_type=jax.ShapeDtypeStruct((n * PEER_SEL,), F32),
        mesh=_sc_mesh(),
        scratch_types=[pltpu.VMEM((SC_TOK * PEER_SEL,), I32),
                       pltpu.VMEM((SC_TOK, d), F32),
                       pltpu.VMEM((2, SC_ROWS, d // 2), I32),
                       pltpu.VMEM((SC_TOK * PEER_SEL,), F32),
                       pltpu.SemaphoreType.DMA, pltpu.SemaphoreType.DMA],
        compiler_params=pltpu.CompilerParams(needs_layout_passes=False),
        name="sc_peer_u",
    )(idx_flat, h, u_tab)


def _sc_v_body(idx_hbm, w_hbm, x_hbm, v_hbm, o_hbm, idx_v, w_v, out_v, rows_v, sem0, sem1, *, tpw, d):
    tok_base = (lax.axis_index("s") * SC_CORES + lax.axis_index("c")) * tpw
    sems = (sem0, sem1)
    half = d // 2
    nword = SC_VCOLS // (2 * SC_LANES)

    def gather(q, b):
        return pltpu.make_async_copy(v_hbm.at[idx_v.at[pl.ds(q * SC_ROWS, SC_ROWS)]], rows_v.at[b], sems[b])

    def compute(q, b):
        tok = q // (PEER_SEL // SC_ROWS)
        for cp in range(d // SC_VCOLS):
            w0 = cp * nword * SC_LANES
            cols = [half * part + w0 + a * SC_LANES for a in range(nword) for part in range(2)]

            def body(r, accs, w0=w0):
                wv = plsc.load_gather(w_v, [jnp.full((SC_LANES,), q * SC_ROWS + r, I32)])
                out = []
                for a in range(nword):
                    lo, hi = _unpack_pair(rows_v[b, r, pl.ds(w0 + a * SC_LANES, SC_LANES)])
                    out += [accs[2 * a] + wv * lo, accs[2 * a + 1] + wv * hi]
                return tuple(out)

            accs = lax.fori_loop(0, SC_ROWS, body,
                                 tuple(out_v[tok, pl.ds(c, SC_LANES)] for c in cols))
            for c, acc in zip(cols, accs):
                out_v[tok, pl.ds(c, SC_LANES)] = acc

    @pl.loop(0, tpw // SC_TOK)
    def _(blk):
        tok0 = tok_base + blk * SC_TOK
        pltpu.sync_copy(idx_hbm.at[pl.ds(tok0 * PEER_SEL, SC_TOK * PEER_SEL)], idx_v)
        pltpu.sync_copy(w_hbm.at[pl.ds(tok0 * PEER_SEL, SC_TOK * PEER_SEL)], w_v)
        pltpu.sync_copy(x_hbm.at[pl.ds(tok0, SC_TOK)], out_v)
        _sc_chunk_pipeline(gather, compute)
        pltpu.sync_copy(out_v, o_hbm.at[pl.ds(tok0, SC_TOK)])


def _sc_peer_v(idx_flat, w_flat, x, v_tab):
    n, d = x.shape
    assert v_tab.shape[1] * 2 == d and v_tab.dtype == I32
    assert n % (SC_WORKERS * SC_TOK) == 0 and d % SC_VCOLS == 0
    body = functools.partial(_sc_v_body, tpw=n // SC_WORKERS, d=d)
    return pl.kernel(
        body,
        out_type=jax.ShapeDtypeStruct((n, d), F32),
        mesh=_sc_mesh(),
        scratch_types=[pltpu.VMEM((SC_TOK * PEER_SEL,), I32),
                       pltpu.VMEM((SC_TOK * PEER_SEL,), F32),
                       pltpu.VMEM((SC_TOK, d), F32),
                       pltpu.VMEM((2, SC_ROWS, d // 2), I32),
                       pltpu.SemaphoreType.DMA, pltpu.SemaphoreType.DMA],
        compiler_params=pltpu.CompilerParams(needs_layout_passes=False),
        name="sc_peer_v",
    )(idx_flat, w_flat, x, v_tab)


def _peer_gate_kernel(g_ref, act_ref, w_ref):
    act = act_ref[...]
    w_ref[...] = g_ref[...] * (0.5 * act * (1.0 + lax.erf(act * (2.0 ** -0.5))))


def _peer_gate(g, act, tm=256):
    n = g.shape[0]
    spec = pl.BlockSpec((tm, PEER_SEL), lambda i: (i, 0))
    return pl.pallas_call(
        _peer_gate_kernel, grid=(n // tm,), in_specs=[spec, spec], out_specs=spec,
        out_shape=jax.ShapeDtypeStruct((n, PEER_SEL), F32),
        compiler_params=_params("parallel"), name="peer_gate",
    )(g, act)


PEER_TC_TOK = 256
PEER_TB = 8


def _tc_row_copy(tab_hbm, rows, sem, e, t, j):
    return pltpu.make_async_copy(tab_hbm.at[pl.ds(e, 1), :], rows.at[t, pl.ds(j, 1), :], sem)


def _tc_gather_rows(idx_ref, tab_hbm, rows, sem):
    for t in range(PEER_TB):
        def issue(j, carry, t=t):
            _tc_row_copy(tab_hbm, rows, sem, idx_ref[t, j], t, j).start()
            return carry
        lax.fori_loop(0, PEER_SEL, issue, 0, unroll=8)
    for t in range(PEER_TB):
        def drain(j, carry, t=t):
            _tc_row_copy(tab_hbm, rows, sem, 0, t, j).wait()
            return carry
        lax.fori_loop(0, PEER_SEL, drain, 0, unroll=8)


def _tc_unpack(words):
    lo = pltpu.bitcast(jnp.left_shift(words, 16), F32)
    hi = pltpu.bitcast(jnp.bitwise_and(words, jnp.int32(-65536)), F32)
    return lo.astype(BF16), hi.astype(BF16)


def _tc_peer_u_kernel(idx_ref, h_ref, g_ref, u_hbm, w_ref, rows, sem):
    _tc_gather_rows(idx_ref, u_hbm, rows, sem)
    half = h_ref.shape[1] // 2
    h = h_ref[...].astype(BF16)
    acts = []
    for t in range(PEER_TB):
        lo, hi = _tc_unpack(rows[t])
        acts.append(_dot_nt(h[t:t + 1, :half], lo) + _dot_nt(h[t:t + 1, half:], hi))
    act = jnp.concatenate(acts, axis=0)
    w_ref[...] = g_ref[...] * (0.5 * act * (1.0 + lax.erf(act * (2.0 ** -0.5))))


def _tc_peer_v_kernel(idx_ref, w_ref, x_ref, v_hbm, o_ref, rows, sem):
    _tc_gather_rows(idx_ref, v_hbm, rows, sem)
    w = w_ref[...].astype(BF16)
    outs = []
    for t in range(PEER_TB):
        lo, hi = _tc_unpack(rows[t])
        outs.append(jnp.concatenate([_dot(w[t:t + 1, :], lo), _dot(w[t:t + 1, :], hi)], axis=1))
    o_ref[...] = x_ref[...] + jnp.concatenate(outs, axis=0)


def _tc_peer_u(idx, h, g, u_tab):
    n, d = h.shape
    sel = pl.BlockSpec((PEER_TB, PEER_SEL), lambda i: (i, 0))
    return pl.pallas_call(
        _tc_peer_u_kernel, grid=(n // PEER_TB,),
        in_specs=[pl.BlockSpec((PEER_TB, PEER_SEL), lambda i: (i, 0), memory_space=pltpu.SMEM),
                  pl.BlockSpec((PEER_TB, d), lambda i: (i, 0)), sel,
                  pl.BlockSpec(memory_space=pl.ANY)],
        out_specs=sel,
        out_shape=jax.ShapeDtypeStruct((n, PEER_SEL), F32),
        scratch_shapes=[pltpu.VMEM((PEER_TB, PEER_SEL, d // 2), I32), pltpu.SemaphoreType.DMA],
        compiler_params=_params("arbitrary"), name="tc_peer_u",
    )(idx, h, g, u_tab)


def _tc_peer_v(idx, w, x, v_tab):
    n, d = x.shape
    return pl.pallas_call(
        _tc_peer_v_kernel, grid=(n // PEER_TB,),
        in_specs=[pl.BlockSpec((PEER_TB, PEER_SEL), lambda i: (i, 0), memory_space=pltpu.SMEM),
                  pl.BlockSpec((PEER_TB, PEER_SEL), lambda i: (i, 0)),
                  pl.BlockSpec((PEER_TB, d), lambda i: (i, 0)),
                  pl.BlockSpec(memory_space=pl.ANY)],
        out_specs=pl.BlockSpec((PEER_TB, d), lambda i: (i, 0)),
        out_shape=jax.ShapeDtypeStruct((n, d), F32),
        scratch_shapes=[pltpu.VMEM((PEER_TB, PEER_SEL, d // 2), I32), pltpu.SemaphoreType.DMA],
        compiler_params=_params("arbitrary"), name="tc_peer_v",
    )(idx, w, x, v_tab)


def _pad_cols(w, width):
    return jnp.pad(w, ((0, 0), (0, width - w.shape[1])))


def _even_weights(w_in):
    hd = A_HEADS * A_HEAD_DIM
    cols = np.cumsum([0, hd, A_HEAD_DIM, A_HEAD_DIM, IDX_HEADS * IDX_DIM, IDX_DIM, IDX_HEADS,
                      B_HEADS * B_HEAD_DIM, B_HEADS * B_HEAD_DIM, B_HEADS * B_HEAD_DIM])
    qa, ka, va, iq, ik, iw, qb, kb, vb = [w_in[:, cols[j]:cols[j + 1]] for j in range(9)]
    vaiw = _pad_cols(jnp.concatenate([va, iw], axis=1), LANES)
    w = jnp.concatenate([qa, iq, qb, kb, vb, ka, ik, vaiw], axis=1).astype(BF16)
    groups = [(0, 512, "r64"), (512, 512, "r64"), (1024, 512, "r64"), (1536, 512, "r64"),
              (2048, 512, None), (2560, LANES, "r64"), (2560 + LANES, LANES, None)]
    return w, groups


def _even_mixer(x, gain, w_in, w_out, tables, batch, s_len):
    w, groups = _even_weights(w_in)
    qa, iq, qb, kb, vb, kaik, vaiw = _fused_proj(x, gain, w, groups, tables)
    out_a = _dsa_attention(iq, qa, kaik, vaiw, batch, s_len)
    out_b = _moba_attention(qb, kb, vb, batch, s_len)
    hd = A_HEADS * A_HEAD_DIM
    w_out = w_out.astype(BF16)
    return _outproj_residual(x, [out_a, out_b], [w_out[:hd], w_out[hd:]])


def _mla_mixer(x, gain, w_in, q_norm, kv_norm, w_uq, w_ukv, w_out, tables, batch, s_len):
    w1 = _pad_cols(w_in, C_Q_RANK + C_KV_RANK + LANES).astype(BF16)
    groups1 = [(0, C_Q_RANK, None), (C_Q_RANK, C_KV_RANK, None), (C_Q_RANK + C_KV_RANK, LANES, "r32")]
    cq, ckv, kr = _fused_proj(x, gain, w1, groups1, tables)
    qd = C_NOPE + C_ROPE
    nope_cols = np.concatenate([np.arange(h * qd, h * qd + C_NOPE) for h in range(C_HEADS)])
    rope_cols = np.concatenate([np.arange(h * qd + C_NOPE, (h + 1) * qd) for h in range(C_HEADS)])
    w2 = jnp.concatenate([w_uq[:, nope_cols], w_uq[:, rope_cols]], axis=1).astype(BF16)
    groups2 = [(0, C_HEADS * C_NOPE, None), (C_HEADS * C_NOPE, C_HEADS * C_ROPE, "r32")]
    qn, qr = _fused_proj(cq, q_norm, w2, groups2, tables)
    kd = C_NOPE + C_V
    kn_cols = np.concatenate([np.arange(h * kd, h * kd + C_NOPE) for h in range(C_HEADS)])
    v_cols = np.concatenate([np.arange(h * kd + C_NOPE, (h + 1) * kd) for h in range(C_HEADS)])
    w3 = jnp.concatenate([w_ukv[:, kn_cols], w_ukv[:, v_cols]], axis=1).astype(BF16)
    groups3 = [(0, C_HEADS * C_NOPE, None), (C_HEADS * C_NOPE, C_HEADS * C_V, None)]
    kn, v = _fused_proj(ckv, kv_norm, w3, groups3, tables)
    o = _mla_attention(qn, qr, kn, kr, v, batch, s_len)
    return _outproj_residual(x, [o], [w_out.astype(BF16)])


def _peer_ffn(x, gain, w_q, sub_keys, u_all, v_all, layer):
    n, d = x.shape
    n_experts = PEER_NKEYS * PEER_NKEYS
    q, h = _fused_proj(x, gain, w_q.astype(BF16), [(0, PEER_HEADS * PEER_QDIM, None)], {}, emit_h=True)
    idx, g = _peer_topk(q, sub_keys.astype(BF16), layer * n_experts)
    ns = n - PEER_TC_TOK
    idx_flat = idx[:ns].reshape(ns * PEER_SEL)
    act = _sc_peer_u(idx_flat, h[:ns], u_all)
    w_tc = _tc_peer_u(idx[ns:], h[ns:], g[ns:], u_all)
    w = _peer_gate(g[:ns], act.reshape(ns, PEER_SEL))
    out_sc = _sc_peer_v(idx_flat, w.reshape(ns * PEER_SEL), x[:ns], v_all)
    out_tc = _tc_peer_v(idx[ns:], w_tc, x[ns:], v_all)
    return jnp.concatenate([out_sc, out_tc], axis=0)


def kernel(x, positions, attn_norm, ffn_norm, final_norm, hyb_w_in, hyb_w_out, mla_w_in, mla_q_norm,
           mla_kv_norm, mla_w_uq, mla_w_ukv, mla_w_out, peer_w_q, peer_sub_keys, peer_u, peer_v):
    batch, s_len, d = x.shape
    depth = attn_norm.shape[0]
    u_all = _pack_table(peer_u.reshape(-1, d))
    v_all = _pack_table(peer_v.reshape(-1, d))
    gb = batch // BATCH_GROUPS
    gn = gb * s_len
    xs, tabs = [], []
    for g in range(BATCH_GROUPS):
        pos = positions[g * gb:(g + 1) * gb].reshape(gn, 1).astype(I32)
        tabs.append({"r64": _rope_tables(pos, A_HEAD_DIM), "r32": _rope_tables(pos, C_ROPE)})
        xs.append(x[g * gb:(g + 1) * gb].reshape(gn, d))
    for i in range(depth):
        j = i // 2
        for g in range(BATCH_GROUPS):
            if i % 2 == 0:
                xs[g] = _even_mixer(xs[g], attn_norm[i], hyb_w_in[j], hyb_w_out[j], tabs[g], gb, s_len)
            else:
                xs[g] = _mla_mixer(xs[g], attn_norm[i], mla_w_in[j], mla_q_norm[j], mla_kv_norm[j],
                                   mla_w_uq[j], mla_w_ukv[j], mla_w_out[j], tabs[g], gb, s_len)
            xs[g] = _peer_ffn(xs[g], ffn_norm[i], peer_w_q[i], peer_sub_keys[i], u_all, v_all, i)
    outs = [_rmsnorm(xg, final_norm).reshape(gb, s_len, d) for xg in xs]
    return jnp.concatenate(outs, axis=0)
```
